```python
import jax, jax.numpy as jnp
from jax import lax
import numpy as np

D_MODEL = 2048
BATCH = 1
SEQ = 16384
DEPTH = 4

N_MIXERS = 4
N_META = 16
D_FF = 5632
DEEPNORM_ALPHA = (2 * DEPTH) ** 0.25
DEEPNORM_BETA = (8 * DEPTH) ** -0.25
LN_EPS = 1e-5
RWKV_HEAD_DIM = 64
RWKV_HEADS = D_MODEL // RWKV_HEAD_DIM
RWKV_DECAY_LORA = max(32, int(round(1.8 * D_MODEL ** 0.5 / 32)) * 32)
RWKV_AAA_LORA = max(32, int(round(1.8 * D_MODEL ** 0.5 / 32)) * 32)
RWKV_GATE_LORA = max(32, int(round(0.6 * D_MODEL ** 0.8 / 32)) * 32)
RWKV_GN_EPS = 64e-5
RG_BLOCKS = 16
D_RNN = ((4 * D_MODEL // 3) // (RG_BLOCKS * 8)) * RG_BLOCKS * 8
RG_BLOCK_DIM = D_RNN // RG_BLOCKS
RG_C = 8.0
CONV_W = 4
FOX_HEAD_DIM = 128
FOX_HEADS = D_MODEL // FOX_HEAD_DIM
FOX_BLOCK = 128
ML_HEADS = 8
ML_QK = D_MODEL // 2
ML_V = D_MODEL
ML_DQK = ML_QK // ML_HEADS
ML_DV = ML_V // ML_HEADS
ML_CHUNK = 64
ML_EPS = 1e-6

kernel_name = 'hybrid_interleaved_rwkv7_rglru_fox_mlstm'


def layer_norm(x, g, b):
    xf = x.astype(jnp.float32)
    mu = jnp.mean(xf, -1, keepdims=True)
    var = jnp.mean(jnp.square(xf - mu), -1, keepdims=True)
    return ((xf - mu) * lax.rsqrt(var + LN_EPS) * g + b).astype(x.dtype)


def post_norm(x, y, g, b):
    return layer_norm(DEEPNORM_ALPHA * x + y, g, b)


def swiglu_ffn(x, w_in, w_out):
    gate, up = jnp.split(x @ w_in, 2, axis=-1)
    return (jax.nn.silu(gate) * up) @ w_out


def token_shift(x):
    return jnp.pad(x, ((0, 0), (1, 0), (0, 0)))[:, :-1]


def rwkv7_time_mix(x, mix, w_rkv, w0, w1, w2, a0, a1, a2, g1, g2, k_k, k_a, r_k, gn_g, gn_b, w_o):
    B, L, _ = x.shape
    H, N = RWKV_HEADS, RWKV_HEAD_DIM
    xx = token_shift(x) - x
    xs = x[None] + xx[None] * mix[:, None, None, :]
    r, k, v = jnp.einsum('sbld,sde->sble', xs[:3], w_rkv)
    w_log = -jax.nn.softplus(-(w0 + jnp.tanh(xs[3] @ w1) @ w2)) - 0.5
    a = jax.nn.sigmoid(a0 + (xs[4] @ a1) @ a2)
    g = jax.nn.sigmoid(xs[5] @ g1) @ g2
    heads = lambda t: t.reshape(B, L, H, N).astype(jnp.float32)
    kk = heads(k * k_k)
    kk = kk * lax.rsqrt(jnp.sum(kk * kk, -1, keepdims=True) + 1e-12)
    k = k * (1 + (a - 1) * k_a)
    r_h, k_h, v_h, a_h = heads(r), heads(k), heads(v), heads(a)
    decay = jnp.exp(-jnp.exp(heads(w_log)))

    def step(S, inp):
        r_t, w_t, k_t, v_t, kk_t, b_t = inp
        sa = jnp.einsum('bhvk,bhk->bhv', S, -kk_t)
        S = S * w_t[:, :, None, :] + sa[..., None] * b_t[:, :, None, :] + v_t[..., None] * k_t[:, :, None, :]
        return S, jnp.einsum('bhvk,bhk->bhv', S, r_t)

    tm = lambda t: jnp.moveaxis(t, 1, 0)
    S0 = jnp.zeros((B, H, N, N), jnp.float32)
    _, y = lax.scan(step, S0, (tm(r_h), tm(decay), tm(k_h), tm(v_h), tm(kk), tm(kk * a_h)))
    y = jnp.moveaxis(y, 0, 1)
    mu = jnp.mean(y, -1, keepdims=True)
    var = jnp.mean(jnp.square(y - mu), -1, keepdims=True)
    y = ((y - mu) * lax.rsqrt(var + RWKV_GN_EPS)).reshape(B, L, D_MODEL) * gn_g + gn_b
    bonus = jnp.sum(r_h * k_h * r_k, -1, keepdims=True) * v_h
    y = (y + bonus.reshape(B, L, D_MODEL)) * g
    return y.astype(x.dtype) @ w_o


def _linear_recurrence_combine(left, right):
    a_l, b_l = left
    a_r, b_r = right
    return (a_l * a_r, a_r * b_l + b_r)


def rglru_block(x, w_in, conv_w, conv_b, w_a, b_a, w_x, b_x, lam, w_o):
    B, L, _ = x.shape
    gate_br, u = jnp.split(x @ w_in, 2, axis=-1)
    up = jnp.pad(u, ((0, 0), (CONV_W - 1, 0), (0, 0)))
    uc = conv_b + sum(up[:, j:j + L] * conv_w[j] for j in range(CONV_W))
    ub = uc.reshape(B, L, RG_BLOCKS, RG_BLOCK_DIM)
    r = jax.nn.sigmoid(jnp.einsum('blnc,ncd->blnd', ub, w_a).reshape(B, L, D_RNN) + b_a)
    i = jax.nn.sigmoid(jnp.einsum('blnc,ncd->blnd', ub, w_x).reshape(B, L, D_RNN) + b_x)
    log_a = (-RG_C * r * jax.nn.softplus(-lam)).astype(jnp.float32)
    a = jnp.exp(log_a)
    b = jnp.sqrt(-jnp.expm1(2.0 * log_a)) * (i * uc).astype(jnp.float32)
    _, hs = lax.associative_scan(_linear_recurrence_combine, (a, b), axis=1)
    y = jax.nn.gelu(gate_br) * hs.astype(x.dtype)
    return y @ w_o


def fox_attention(x, w_in, b_f, w_o):
    B, L, _ = x.shape
    H, Dh = FOX_HEADS, FOX_HEAD_DIM
    q, k, v, fl = jnp.split(x @ w_in, [D_MODEL, 2 * D_MODEL, 3 * D_MODEL], axis=-1)
    heads = lambda t: t.reshape(B, L, H, Dh).transpose(0, 2, 1, 3)
    q, k, v = heads(q), heads(k), heads(v)
    logf = jax.nn.log_sigmoid((fl + b_f).astype(jnp.float32))
    c = jnp.cumsum(logf, axis=1).transpose(0, 2, 1)
    scale = Dh ** -0.5
    pos = jnp.arange(L)

    def attend(qb, cq, q_pos, kk, vv, ck, k_pos):
        s = jnp.einsum('bhqd,bhkd->bhqk', qb, kk).astype(jnp.float32) * scale
        s = s + cq[..., :, None] - ck[..., None, :]
        s = jnp.where(k_pos[None, :] <= q_pos[:, None], s, -jnp.inf)
        p = jax.nn.softmax(s, axis=-1)
        return jnp.einsum('bhqk,bhkd->bhqd', p.astype(vv.dtype), vv)

    m = N_META
    o_meta = attend(q[:, :, :m], c[:, :, :m], pos[:m], k[:, :, :m], v[:, :, :m], c[:, :, :m], pos[:m])
    nb = (L - m) // FOX_BLOCK
    qb = q[:, :, m:].reshape(B, H, nb, FOX_BLOCK, Dh).transpose(2, 0, 1, 3, 4)
    cb = c[:, :, m:].reshape(B, H, nb, FOX_BLOCK).transpose(2, 0, 1, 3)

    def block(args):
        qi, ci, bi = args
        q_pos = m + bi * FOX_BLOCK + jnp.arange(FOX_BLOCK)
        return attend(qi, ci, q_pos, k, v, c, pos)

    o_real = lax.map(block, (qb, cb, jnp.arange(nb)))
    o_real = o_real.transpose(1, 2, 0, 3, 4).reshape(B, H, L - m, Dh)
    o = jnp.concatenate([o_meta, o_real], axis=2).transpose(0, 2, 1, 3).reshape(B, L, D_MODEL)
    return o @ w_o


def mlstm_chunk(state, inp):
    C, n, m = state
    q, k, v, ig, lf = inp
    Lc = q.shape[2]
    b = jnp.cumsum(lf, axis=-1)
    causal = jnp.tril(jnp.ones((Lc, Lc), dtype=bool))
    dmat = jnp.where(causal, b[..., :, None] - b[..., None, :] + ig[..., None, :], -jnp.inf)
    inter = b + m[..., None]
    m_t = jnp.maximum(inter, jnp.max(dmat, axis=-1))
    w_intra = jnp.exp(dmat - m_t[..., None])
    w_inter = jnp.exp(inter - m_t)
    s = jnp.einsum('bhtd,bhsd->bhts', q, k) * w_intra
    num = jnp.einsum('bhts,bhsv->bhtv', s, v) + w_inter[..., None] * jnp.einsum('bhtd,bhdv->bhtv', q, C)
    den = jnp.sum(s, -1) + w_inter * jnp.einsum('bhtd,bhd->bht', q, n)
    h = num / jnp.maximum(jnp.abs(den), jnp.exp(-m_t))[..., None]
    b_end = b[..., -1]
    d_end = b_end[..., None] - b + ig
    m_new = jnp.maximum(b_end + m, jnp.max(d_end, axis=-1))
    w_s = jnp.exp(d_end - m_new[..., None])
    carry = jnp.exp(b_end + m - m_new)
    C_new = carry[..., None, None] * C + jnp.einsum('bhs,bhsd,bhsv->bhdv', w_s, k, v)
    n_new = carry[..., None] * n + jnp.einsum('bhs,bhsd->bhd', w_s, k)
    return (C_new, n_new, m_new), h


def mlstm_block(x, w_in, b_if, norm_g, w_o):
    B, L, _ = x.shape
    H = ML_HEADS
    splits = [ML_QK, 2 * ML_QK, 2 * ML_QK + ML_V, 2 * ML_QK + 2 * ML_V, 2 * ML_QK + 2 * ML_V + H]
    q, k, v, o, ig, fg = jnp.split(x @ w_in, splits, axis=-1)
    heads = lambda t, d: t.reshape(B, L, H, d).transpose(0, 2, 1, 3).astype(jnp.float32)
    q = heads(q, ML_DQK)
    k = heads(k, ML_DQK) * (ML_DQK ** -0.5)
    v = heads(v, ML_DV)
    ig = (ig + b_if[0]).astype(jnp.float32).transpose(0, 2, 1)
    lf = jax.nn.log_sigmoid((fg + b_if[1]).astype(jnp.float32)).transpose(0, 2, 1)
    state0 = (jnp.zeros((B, H, ML_DQK, ML_DV), jnp.float32),
              jnp.zeros((B, H, ML_DQK), jnp.float32),
              jnp.zeros((B, H), jnp.float32))
    m = N_META
    state, h_meta = mlstm_chunk(state0, (q[:, :, :m], k[:, :, :m], v[:, :, :m], ig[:, :, :m], lf[:, :, :m]))
    nc = (L - m) // ML_CHUNK
    chunks = lambda t: jnp.moveaxis(t[:, :, m:].reshape((B, H, nc, ML_CHUNK) + t.shape[3:]), 2, 0)
    _, h_real = lax.scan(mlstm_chunk, state, (chunks(q), chunks(k), chunks(v), chunks(ig), chunks(lf)))
    h_real = jnp.moveaxis(h_real, 0, 2).reshape(B, H, L - m, ML_DV)
    h = jnp.concatenate([h_meta, h_real], axis=2)
    h = h * lax.rsqrt(jnp.mean(jnp.square(h), -1, keepdims=True) + ML_EPS)
    h = h.transpose(0, 2, 1, 3).reshape(B, L, ML_V) * norm_g
    y = (h * jax.nn.sigmoid(o)).astype(x.dtype)
    return y @ w_o


def setup_inputs(seed: int = 0) -> dict:
    key = jax.random.key(seed)
    ks = jax.random.split(key, 48)
    counter = [0]

    def nxt():
        counter[0] += 1
        return ks[counter[0] - 1]

    def nrm(shape, scale):
        return jax.random.normal(nxt(), shape, jnp.float32) * scale

    def unif(shape, lo, hi):
        return jax.random.uniform(nxt(), shape, jnp.float32, lo, hi)

    D = D_MODEL
    beta = DEEPNORM_BETA
    a_target = unif((D_RNN,), 0.9, 0.999)
    s_lam = a_target ** (1.0 / RG_C)
    return {
        'x': nrm((BATCH, SEQ, D), 1.0),
        'meta_tokens': nrm((N_META, D), 1.0),
        'ln_g': 1.0 + nrm((DEPTH, 3, D), 0.02),
        'ln_b': nrm((DEPTH, 3, D), 0.02),
        'ffn_in': nrm((DEPTH, 2, D, 2 * D_FF), D ** -0.5),
        'ffn_out': nrm((DEPTH, 2, D_FF, D), beta * D_FF ** -0.5),
        'rwkv_mix': unif((6, D), 0.0, 1.0),
        'rwkv_w_rkv': nrm((3, D, D), D ** -0.5),
        'rwkv_w0': unif((D,), -6.5, -1.5),
        'rwkv_w1': nrm((D, RWKV_DECAY_LORA), D ** -0.5),
        'rwkv_w2': nrm((RWKV_DECAY_LORA, D), 0.1 * RWKV_DECAY_LORA ** -0.5),
        'rwkv_a0': nrm((D,), 0.1),
        'rwkv_a1': nrm((D, RWKV_AAA_LORA), D ** -0.5),
        'rwkv_a2': nrm((RWKV_AAA_LORA, D), 0.1 * RWKV_AAA_LORA ** -0.5),
        'rwkv_g1': nrm((D, RWKV_GATE_LORA), D ** -0.5),
        'rwkv_g2': nrm((RWKV_GATE_LORA, D), RWKV_GATE_LORA ** -0.5),
        'rwkv_k_k': 0.85 + nrm((D,), 0.05),
        'rwkv_k_a': 1.0 + nrm((D,), 0.05),
        'rwkv_r_k': nrm((RWKV_HEADS, RWKV_HEAD_DIM), 0.1),
        'rwkv_gn_g': 1.0 + nrm((D,), 0.02),
        'rwkv_gn_b': nrm((D,), 0.02),
        'rwkv_w_o': nrm((D, D), beta * D ** -0.5),
        'rg_w_in': nrm((D, 2 * D_RNN), D ** -0.5),
        'rg_conv_w': nrm((CONV_W, D_RNN), 0.5),
        'rg_conv_b': nrm((D_RNN,), 0.02),
        'rg_w_a': nrm((RG_BLOCKS, RG_BLOCK_DIM, RG_BLOCK_DIM), RG_BLOCK_DIM ** -0.5),
        'rg_b_a': nrm((D_RNN,), 0.1),
        'rg_w_x': nrm((RG_BLOCKS, RG_BLOCK_DIM, RG_BLOCK_DIM), RG_BLOCK_DIM ** -0.5),
        'rg_b_x': nrm((D_RNN,), 0.1),
        'rg_lambda': jnp.log(s_lam) - jnp.log1p(-s_lam),
        'rg_w_o': nrm((D_RNN, D), beta * D_RNN ** -0.5),
        'fox_w_in': nrm((D, 3 * D + FOX_HEADS), D ** -0.5),
        'fox_b_f': 2.0 + nrm((FOX_HEADS,), 0.5),
        'fox_w_o': nrm((D, D), beta * D ** -0.5),
        'ml_w_in': nrm((D, 2 * ML_QK + 2 * ML_V + 2 * ML_HEADS), D ** -0.5),
        'ml_b_if': jnp.stack([nrm((ML_HEADS,), 0.1), 3.0 + nrm((ML_HEADS,), 0.5)]),
        'ml_norm_g': 1.0 + nrm((ML_V,), 0.02),
        'ml_w_o': nrm((ML_V, D), beta * ML_V ** -0.5),
    }


def reference(x, meta_tokens, ln_g, ln_b, ffn_in, ffn_out,
              rwkv_mix, rwkv_w_rkv, rwkv_w0, rwkv_w1, rwkv_w2, rwkv_a0, rwkv_a1, rwkv_a2,
              rwkv_g1, rwkv_g2, rwkv_k_k, rwkv_k_a, rwkv_r_k, rwkv_gn_g, rwkv_gn_b, rwkv_w_o,
              rg_w_in, rg_conv_w, rg_conv_b, rg_w_a, rg_b_a, rg_w_x, rg_b_x, rg_lambda, rg_w_o,
              fox_w_in, fox_b_f, fox_w_o,
              ml_w_in, ml_b_if, ml_norm_g, ml_w_o):
    B = x.shape[0]
    meta = jnp.broadcast_to(meta_tokens.astype(x.dtype)[None], (B, N_META, D_MODEL))
    h = jnp.concatenate([meta, x], axis=1)
    for layer in range(DEPTH):
        h = post_norm(h, 0.5 * swiglu_ffn(h, ffn_in[layer, 0], ffn_out[layer, 0]), ln_g[layer, 0], ln_b[layer, 0])
        kind = layer % N_MIXERS
        if kind == 0:
            y = rwkv7_time_mix(h, rwkv_mix, rwkv_w_rkv, rwkv_w0, rwkv_w1, rwkv_w2, rwkv_a0, rwkv_a1, rwkv_a2,
                               rwkv_g1, rwkv_g2, rwkv_k_k, rwkv_k_a, rwkv_r_k, rwkv_gn_g, rwkv_gn_b, rwkv_w_o)
        elif kind == 1:
            y = rglru_block(h, rg_w_in, rg_conv_w, rg_conv_b, rg_w_a, rg_b_a, rg_w_x, rg_b_x, rg_lambda, rg_w_o)
        elif kind == 2:
            y = fox_attention(h, fox_w_in, fox_b_f, fox_w_o)
        else:
            y = mlstm_block(h, ml_w_in, ml_b_if, ml_norm_g, ml_w_o)
        h = post_norm(h, y, ln_g[layer, 1], ln_b[layer, 1])
        h = post_norm(h, 0.5 * swiglu_ffn(h, ffn_in[layer, 1], ffn_out[layer, 1]), ln_g[layer, 2], ln_b[layer, 2])
    return h[:, N_META:]
```

```python
import functools

import numpy as np
import jax
import jax.numpy as jnp
from jax import lax
from jax.experimental import pallas as pl
from jax.experimental.pallas import tpu as pltpu

F32 = jnp.float32
BF16 = jnp.bfloat16

D_MODEL = 2048
DEPTH = 4
N_META = 16
D_FF = 5632
ALPHA = (2 * DEPTH) ** 0.25
LN_EPS = 1e-5
RWKV_HEAD_DIM = 64
RWKV_HEADS = D_MODEL // RWKV_HEAD_DIM
RWKV_GN_EPS = 64e-5
RG_BLOCKS = 16
D_RNN = 2688
RG_BLOCK_DIM = D_RNN // RG_BLOCKS
RG_C = 8.0
CONV_W = 4
FOX_HEAD_DIM = 128
FOX_HEADS = D_MODEL // FOX_HEAD_DIM
ML_HEADS = 8
ML_QK = D_MODEL // 2
ML_V = D_MODEL
ML_DQK = ML_QK // ML_HEADS
ML_DV = ML_V // ML_HEADS
ML_EPS = 1e-6

LANE = 128
SUBLANE = 8
ROW_ALIGN = 1280
VMEM_LIMIT = 56 * 1024 * 1024

FFN_TM, FFN_TN = 640, 512
PROJ_TM = 640
OUT_TM = 256
RWKV_PREP_TM, RWKV_PREP_TN = 256, 512
RWKV_CHUNK = 64
RG_TM = 256
RG_KWIN = 512
FOX_TQ = 256
ML_CHUNK = 256

HI = lax.Precision.HIGHEST


def _cparams(sem):
    return pltpu.CompilerParams(dimension_semantics=sem, vmem_limit_bytes=VMEM_LIMIT)


def _ln(y, g, b):
    mu = jnp.mean(y, -1, keepdims=True)
    yc = y - mu
    var = jnp.mean(yc * yc, -1, keepdims=True)
    return yc * lax.rsqrt(var + LN_EPS) * g + b


def _softplus(z):
    return jnp.maximum(z, 0.0) + jnp.log1p(jnp.exp(-jnp.abs(z)))


def _log_sigmoid(z):
    return -_softplus(-z)


def _scan_add(x, axis):
    n = x.shape[axis]
    idx = lax.broadcasted_iota(jnp.int32, x.shape, axis)
    s = 1
    while s < n:
        x = x + jnp.where(idx >= s, pltpu.roll(x, s, axis), 0.0)
        s *= 2
    return x


def _ffn_ln_body(x_ref, wg_ref, wu_ref, wo_ref, g_ref, b_ref, o_ref, xb_ref, acc_ref):
    j = pl.program_id(1)

    @pl.when(j == 0)
    def _():
        xb_ref[...] = x_ref[...].astype(BF16)
        acc_ref[...] = jnp.zeros_like(acc_ref)

    xb = xb_ref[...]
    hg = jnp.dot(xb, wg_ref[...], preferred_element_type=F32)
    hu = jnp.dot(xb, wu_ref[...], preferred_element_type=F32)
    act = (hg * jax.nn.sigmoid(hg) * hu).astype(BF16)
    acc_ref[...] += jnp.dot(act, wo_ref[...], preferred_element_type=F32)

    @pl.when(j == pl.num_programs(1) - 1)
    def _():
        y = ALPHA * x_ref[...] + 0.5 * acc_ref[...]
        o_ref[...] = _ln(y, g_ref[...], b_ref[...])


def ffn_ln(h, w_in, w_out, layer, idx, g, b):
    lp = h.shape[0]
    tm, tn = FFN_TM, FFN_TN
    nf = D_FF // tn
    return pl.pallas_call(
        _ffn_ln_body,
        grid=(lp // tm, nf),
        in_specs=[
            pl.BlockSpec((tm, D_MODEL), lambda i, j: (i, 0)),
            pl.BlockSpec((None, None, D_MODEL, tn), lambda i, j: (layer, idx, 0, j)),
            pl.BlockSpec((None, None, D_MODEL, tn), lambda i, j: (layer, idx, 0, j + nf)),
            pl.BlockSpec((None, None, tn, D_MODEL), lambda i, j: (layer, idx, j, 0)),
            pl.BlockSpec((1, D_MODEL), lambda i, j: (0, 0)),
            pl.BlockSpec((1, D_MODEL), lambda i, j: (0, 0)),
        ],
        out_specs=pl.BlockSpec((tm, D_MODEL), lambda i, j: (i, 0)),
        out_shape=jax.ShapeDtypeStruct((lp, D_MODEL), F32),
        scratch_shapes=[pltpu.VMEM((tm, D_MODEL), BF16), pltpu.VMEM((tm, D_MODEL), F32)],
        compiler_params=_cparams(("parallel", "arbitrary")),
        name="ffn_ln",
    )(h, w_in, w_in, w_out, g, b)


def _proj_body(x_ref, w_ref, o_ref, xb_ref):
    @pl.when(pl.program_id(1) == 0)
    def _():
        xb_ref[...] = x_ref[...].astype(BF16)

    o_ref[...] = jnp.dot(xb_ref[...], w_ref[...], preferred_element_type=F32).astype(o_ref.dtype)


def proj(h, w, tn, out_dtype):
    lp, n = h.shape[0], w.shape[1]
    tm = PROJ_TM
    return pl.pallas_call(
        _proj_body,
        grid=(lp // tm, n // tn),
        in_specs=[pl.BlockSpec((tm, D_MODEL), lambda i, j: (i, 0)),
                  pl.BlockSpec((D_MODEL, tn), lambda i, j: (0, j))],
        out_specs=pl.BlockSpec((tm, tn), lambda i, j: (i, j)),
        out_shape=jax.ShapeDtypeStruct((lp, n), out_dtype),
        scratch_shapes=[pltpu.VMEM((tm, D_MODEL), BF16)],
        compiler_params=_cparams(("parallel", "arbitrary")),
        name="proj",
    )(h, w)


def _outproj_ln_body(x_ref, y_ref, w_ref, g_ref, b_ref, o_ref):
    acc = jnp.dot(y_ref[...], w_ref[...], preferred_element_type=F32)
    o_ref[...] = _ln(ALPHA * x_ref[...] + acc, g_ref[...], b_ref[...])


def outproj_ln(h, y, w, g, b):
    lp, k = y.shape
    tm = OUT_TM
    return pl.pallas_call(
        _outproj_ln_body,
        grid=(lp // tm,),
        in_specs=[pl.BlockSpec((tm, D_MODEL), lambda i: (i, 0)),
                  pl.BlockSpec((tm, k), lambda i: (i, 0)),
                  pl.BlockSpec((k, D_MODEL), lambda i: (0, 0)),
                  pl.BlockSpec((1, D_MODEL), lambda i: (0, 0)),
                  pl.BlockSpec((1, D_MODEL), lambda i: (0, 0))],
        out_specs=pl.BlockSpec((tm, D_MODEL), lambda i: (i, 0)),
        out_shape=jax.ShapeDtypeStruct((lp, D_MODEL), F32),
        compiler_params=_cparams(("parallel",)),
        name="outproj_ln",
    )(h, y, w, g, b)


def _rwkv_prep_body(x_ref, xp_ref, mix_ref, wrkv_ref, w1_ref, a1_ref, g1_ref, w2_ref, a2_ref, g2_ref,
                    w0_ref, a0_ref, kk_ref, ka_ref, bd_ref,
                    r_o, lw_o, k_o, v_o, kk_o, b_o, g_o,
                    xe_ref, xs_ref, tw_ref, ta_ref, tg_ref):
    i, j = pl.program_id(0), pl.program_id(1)
    tm = x_ref.shape[0]

    @pl.when(j == 0)
    def _():
        x = x_ref[...]
        xe_ref[0:SUBLANE, :] = jnp.where(i == 0, 0.0, xp_ref[...])
        xe_ref[SUBLANE:, :] = x
        xx = xe_ref[pl.ds(SUBLANE - 1, tm), :] - x
        mix = mix_ref[...]
        for s in range(3):
            xs_ref[s] = (x + xx * mix[s:s + 1, :]).astype(BF16)
        xs_w = (x + xx * mix[3:4, :]).astype(BF16)
        xs_a = (x + xx * mix[4:5, :]).astype(BF16)
        xs_g = (x + xx * mix[5:6, :]).astype(BF16)
        tw_ref[...] = jnp.tanh(jnp.dot(xs_w, w1_ref[...], preferred_element_type=F32)).astype(BF16)
        ta_ref[...] = jnp.dot(xs_a, a1_ref[...], preferred_element_type=F32).astype(BF16)
        tg_ref[...] = jax.nn.sigmoid(jnp.dot(xs_g, g1_ref[...], preferred_element_type=F32)).astype(BF16)

    r = jnp.dot(xs_ref[0], wrkv_ref[0], preferred_element_type=F32)
    k = jnp.dot(xs_ref[1], wrkv_ref[1], preferred_element_type=F32)
    v = jnp.dot(xs_ref[2], wrkv_ref[2], preferred_element_type=F32)
    wz = w0_ref[...] + jnp.dot(tw_ref[...], w2_ref[...], preferred_element_type=F32)
    az = a0_ref[...] + jnp.dot(ta_ref[...], a2_ref[...], preferred_element_type=F32)
    g = jnp.dot(tg_ref[...], g2_ref[...], preferred_element_type=F32)

    w_log = -_softplus(-wz) - 0.5
    a = jax.nn.sigmoid(az)
    kk = k * kk_ref[...]
    kk2 = kk * kk
    bd = bd_ref[...]
    ss = jnp.concatenate(
        [jnp.dot(kk2[:, c * LANE:(c + 1) * LANE], bd, precision=HI, preferred_element_type=F32)
         for c in range(kk2.shape[1] // LANE)], axis=1)
    kkn = kk * lax.rsqrt(ss + 1e-12)

    r_o[...] = r
    lw_o[...] = -jnp.exp(w_log)
    k_o[...] = k * (1.0 + (a - 1.0) * ka_ref[...])
    v_o[...] = v
    kk_o[...] = kkn
    b_o[...] = kkn * a
    g_o[...] = g


def rwkv_prep(h, p):
    lp = h.shape[0]
    tm, tn = RWKV_PREP_TM, RWKV_PREP_TN
    row = lambda i, j: (i, 0)
    col = lambda i, j: (0, j)
    full = lambda i, j: (0, 0)
    lw, la, lg = p["w1"].shape[1], p["a1"].shape[1], p["g1"].shape[1]
    out = jax.ShapeDtypeStruct((lp, D_MODEL), F32)
    return pl.pallas_call(
        _rwkv_prep_body,
        grid=(lp // tm, D_MODEL // tn),
        in_specs=[
            pl.BlockSpec((tm, D_MODEL), row),
            pl.BlockSpec((SUBLANE, D_MODEL), lambda i, j: (jnp.maximum(i * (tm // SUBLANE) - 1, 0), 0)),
            pl.BlockSpec((SUBLANE, D_MODEL), full),
            pl.BlockSpec((3, D_MODEL, tn), lambda i, j: (0, 0, j)),
            pl.BlockSpec((D_MODEL, lw), full),
            pl.BlockSpec((D_MODEL, la), full),
            pl.BlockSpec((D_MODEL, lg), full),
            pl.BlockSpec((lw, tn), col),
            pl.BlockSpec((la, tn), col),
            pl.BlockSpec((lg, tn), col),
            pl.BlockSpec((1, tn), col),
            pl.BlockSpec((1, tn), col),
            pl.BlockSpec((1, tn), col),
            pl.BlockSpec((1, tn), col),
            pl.BlockSpec((LANE, LANE), full),
        ],
        out_specs=[pl.BlockSpec((tm, tn), lambda i, j: (i, j))] * 7,
        out_shape=[out] * 7,
        scratch_shapes=[pltpu.VMEM((tm + SUBLANE, D_MODEL), F32),
                        pltpu.VMEM((3, tm, D_MODEL), BF16),
                        pltpu.VMEM((tm, lw), BF16), pltpu.VMEM((tm, la), BF16), pltpu.VMEM((tm, lg), BF16)],
        compiler_params=_cparams(("parallel", "arbitrary")),
        name="rwkv_prep",
    )(h, h, p["mix"], p["w_rkv"], p["w1"], p["a1"], p["g1"], p["w2"], p["a2"], p["g2"],
      p["w0"], p["a0"], p["k_k"], p["k_a"], p["bd"])


RWKV_MXU_BF16 = True


def _rw_mm(a, b, dims):
    if RWKV_MXU_BF16:
        return lax.dot_general(a.astype(BF16), b.astype(BF16), (dims, ((), ())), preferred_element_type=F32)
    return lax.dot_general(a, b, (dims, ((), ())), precision=HI, preferred_element_type=F32)


_NN = ((1,), (0,))
_NT = ((1,), (1,))
_TN = ((0,), (0,))


def _rwkv_chunk_body(r_ref, lw_ref, k_ref, v_ref, kk_ref, b_ref, g_ref, rk_ref, gng_ref, gnb_ref, tri_ref,
                     o_ref, s_ref):
    c = r_ref.shape[0]
    n = RWKV_HEAD_DIM

    @pl.when(pl.program_id(0) == 0)
    def _():
        s_ref[...] = jnp.zeros_like(s_ref)

    lw = lw_ref[...]
    cum = jnp.dot(tri_ref[...], lw, precision=HI, preferred_element_type=F32)
    e_pos = jnp.exp(cum)
    e_neg = jnp.exp(-cum)
    e_end = e_pos[c - 1:c, :]
    at_all = -kk_ref[...] * jnp.exp(cum - lw)
    rt_all = r_ref[...] * e_pos
    bb_all = b_ref[...] * e_neg
    kb_all = k_ref[...] * e_neg

    row = lax.broadcasted_iota(jnp.int32, (c, c), 0)
    colm = lax.broadcasted_iota(jnp.int32, (c, c), 1)
    strict = colm < row
    incl = colm <= row

    for h in range(RWKV_HEADS):
        sl = slice(h * n, (h + 1) * n)
        v_h = v_ref[:, sl]
        lhs = jnp.concatenate([at_all[:, sl], rt_all[:, sl]], axis=0)
        rhs = jnp.concatenate([bb_all[:, sl], kb_all[:, sl]], axis=0)
        gram = _rw_mm(lhs, rhs, _NT)
        a_ab = jnp.where(strict, gram[:c, :c], 0.0)
        a_ak = jnp.where(strict, gram[:c, c:], 0.0)
        a_rb = jnp.where(incl, gram[c:, :c], 0.0)
        a_rk = jnp.where(incl, gram[c:, c:], 0.0)
        s0 = s_ref[h]
        z = _rw_mm(lhs, s0, _NT)
        u = z[:c] + _rw_mm(a_ak, v_h, _NN)
        pw = a_ab
        span = 1
        while span < c:
            u = u + _rw_mm(pw, u, _NN)
            span *= 2
            if span < c:
                pw = _rw_mm(pw, pw, _NN)
        uv = jnp.concatenate([u, v_h], axis=0)
        y = z[c:] + _rw_mm(jnp.concatenate([a_rb, a_rk], axis=1), uv, _NN)
        e_end_h = e_end[:, sl]
        s_ref[h] = s0 * e_end_h + _rw_mm(uv, rhs * e_end_h, _TN)

        mu = jnp.mean(y, -1, keepdims=True)
        yc = y - mu
        var = jnp.mean(yc * yc, -1, keepdims=True)
        yn = yc * lax.rsqrt(var + RWKV_GN_EPS) * gng_ref[:, sl] + gnb_ref[:, sl]
        bonus = jnp.sum(r_ref[:, sl] * k_ref[:, sl] * rk_ref[:, sl], -1, keepdims=True) * v_h
        o_ref[:, sl] = ((yn + bonus) * g_ref[:, sl]).astype(o_ref.dtype)


def rwkv_chunk(r, lw, k, v, kk, b, g, r_k, gn_g, gn_b):
    lp = r.shape[0]
    c = RWKV_CHUNK
    tri = jnp.asarray(np.tril(np.ones((c, c), np.float32)))
    blk = pl.BlockSpec((c, D_MODEL), lambda i: (i, 0))
    vec = pl.BlockSpec((1, D_MODEL), lambda i: (0, 0))
    return pl.pallas_call(
        _rwkv_chunk_body,
        grid=(lp // c,),
        in_specs=[blk] * 7 + [vec] * 3 + [pl.BlockSpec((c, c), lambda i: (0, 0))],
        out_specs=blk,
        out_shape=jax.ShapeDtypeStruct((lp, D_MODEL), BF16),
        scratch_shapes=[pltpu.VMEM((RWKV_HEADS, RWKV_HEAD_DIM, RWKV_HEAD_DIM), F32)],
        compiler_params=_cparams(("arbitrary",)),
        name="rwkv_chunk",
    )(r, lw, k, v, kk, b, g, r_k, gn_g, gn_b, tri)


def _pad_rows(a, rows):
    return jnp.pad(a, ((0, rows - a.shape[0]), (0, 0)))


def _pad_cols(a, cols):
    return jnp.pad(a, ((0, 0), (0, cols - a.shape[1])))


def rwkv_params(mix, w_rkv, w0, w1, w2, a0, a1, a2, g1, g2, k_k, k_a, r_k, gn_g, gn_b, w_o):
    lora = LANE
    bd = np.kron(np.eye(LANE // RWKV_HEAD_DIM, dtype=np.float32),
                 np.ones((RWKV_HEAD_DIM, RWKV_HEAD_DIM), np.float32))
    row = lambda t: t.reshape(1, D_MODEL)
    return dict(
        mix=_pad_rows(mix, SUBLANE), w_rkv=w_rkv.astype(BF16),
        w1=_pad_cols(w1, lora).astype(BF16), w2=_pad_rows(w2, lora).astype(BF16),
        a1=_pad_cols(a1, lora).astype(BF16), a2=_pad_rows(a2, lora).astype(BF16),
        g1=g1.astype(BF16), g2=g2.astype(BF16),
        w0=row(w0), a0=row(a0), k_k=row(k_k), k_a=row(k_a), bd=jnp.asarray(bd),
        r_k=row(r_k), gn_g=row(gn_g), gn_b=row(gn_b), w_o=w_o.astype(BF16))


def rwkv_mixer(h, p):
    r, lw, k, v, kk, b, g = rwkv_prep(h, p)
    return rwkv_chunk(r, lw, k, v, kk, b, g, p["r_k"], p["gn_g"], p["gn_b"])


def _rg_windows():
    starts = []
    for c in range(D_RNN // LANE):
        first_block = (c * LANE) // RG_BLOCK_DIM
        ks = (first_block * RG_BLOCK_DIM) // LANE * LANE
        starts.append(min(ks, D_RNN - RG_KWIN))
    return starts


def _rg_compact(w):
    dense = jnp.zeros((D_RNN, D_RNN), F32)
    for nblk in range(RG_BLOCKS):
        s = nblk * RG_BLOCK_DIM
        dense = lax.dynamic_update_slice(dense, w[nblk], (s, s))
    tiles = [dense[ks:ks + RG_KWIN, c * LANE:(c + 1) * LANE] for c, ks in enumerate(_rg_windows())]
    return jnp.stack(tiles).astype(BF16)


def _gelu_tanh(x):
    return 0.5 * x * (1.0 + jnp.tanh(np.sqrt(2.0 / np.pi).astype(np.float32) * (x + 0.044715 * (x * x * x))))


def _rg_body(gate_ref, u_ref, up_ref, cw_ref, cb_ref, wa_ref, wx_ref, ba_ref, bx_ref, lam_ref, o_ref,
             ue_ref, carry_ref):
    i = pl.program_id(0)
    tm = u_ref.shape[0]

    @pl.when(i == 0)
    def _():
        carry_ref[...] = jnp.zeros_like(carry_ref)

    u = u_ref[...]
    ue_ref[0:SUBLANE, :] = jnp.where(i == 0, 0.0, up_ref[...])
    ue_ref[SUBLANE:, :] = u
    cw = cw_ref[...]
    uc = cb_ref[...] + u * cw[CONV_W - 1:CONV_W, :]
    for d in range(1, CONV_W):
        uc = uc + ue_ref[pl.ds(SUBLANE - d, tm), :] * cw[CONV_W - 1 - d:CONV_W - d, :]
    ucb = uc.astype(BF16)
    za, zx = [], []
    for c, ks in enumerate(_rg_windows()):
        win = ucb[:, ks:ks + RG_KWIN]
        za.append(jnp.dot(win, wa_ref[c], preferred_element_type=F32))
        zx.append(jnp.dot(win, wx_ref[c], preferred_element_type=F32))
    r = jax.nn.sigmoid(jnp.concatenate(za, axis=1) + ba_ref[...])
    ig = jax.nn.sigmoid(jnp.concatenate(zx, axis=1) + bx_ref[...])
    log_a = -RG_C * r * _softplus(-lam_ref[...])
    a = jnp.exp(log_a)
    th = jnp.tanh(log_a)
    b = jnp.sqrt(-2.0 * th / (1.0 - th)) * (ig * uc)
    rowi = lax.broadcasted_iota(jnp.int32, a.shape, 0)
    s = 1
    while s < tm:
        keep = rowi >= s
        b = jnp.where(keep, a * pltpu.roll(b, s, 0) + b, b)
        a = jnp.where(keep, a * pltpu.roll(a, s, 0), a)
        s *= 2
    hs = a * carry_ref[0:1, :] + b
    carry_ref[...] = jnp.broadcast_to(hs[tm - 1:tm, :], carry_ref.shape)
    o_ref[...] = (_gelu_tanh(gate_ref[...]) * hs).astype(o_ref.dtype)


def rg_params(w_in, conv_w, conv_b, w_a, b_a, w_x, b_x, lam, w_o):
    row = lambda t: t.reshape(1, D_RNN)
    return dict(w_in=w_in.astype(BF16), conv_w=_pad_rows(conv_w, SUBLANE), conv_b=row(conv_b),
                w_a=_rg_compact(w_a), w_x=_rg_compact(w_x), b_a=row(b_a), b_x=row(b_x),
                lam=row(lam), w_o=w_o.astype(BF16))


def rg_mixer(h, p):
    lp = h.shape[0]
    tm = RG_TM
    gu = proj(h, p["w_in"], 768, F32)
    ntile = D_RNN // LANE
    vec = pl.BlockSpec((1, D_RNN), lambda i: (0, 0))
    wspec = pl.BlockSpec((ntile, RG_KWIN, LANE), lambda i: (0, 0, 0))
    return pl.pallas_call(
        _rg_body,
        grid=(lp // tm,),
        in_specs=[pl.BlockSpec((tm, D_RNN), lambda i: (i, 0)),
                  pl.BlockSpec((tm, D_RNN), lambda i: (i, 1)),
                  pl.BlockSpec((SUBLANE, D_RNN), lambda i: (jnp.maximum(i * (tm // SUBLANE) - 1, 0), 1)),
                  pl.BlockSpec((SUBLANE, D_RNN), lambda i: (0, 0)),
                  vec, wspec, wspec, vec, vec, vec],
        out_specs=pl.BlockSpec((tm, D_RNN), lambda i: (i, 0)),
        out_shape=jax.ShapeDtypeStruct((lp, D_RNN), BF16),
        scratch_shapes=[pltpu.VMEM((tm + SUBLANE, D_RNN), F32), pltpu.VMEM((SUBLANE, D_RNN), F32)],
        compiler_params=_cparams(("arbitrary",)),
        name="rg_lru",
    )(gu, gu, gu, p["conv_w"], p["conv_b"], p["w_a"], p["w_x"], p["b_a"], p["b_x"], p["lam"])


def _fox_gate_body(x_ref, w_ref, bf_ref, o_ref, carry_ref):
    @pl.when(pl.program_id(0) == 0)
    def _():
        carry_ref[...] = jnp.zeros_like(carry_ref)

    fl = jnp.dot(x_ref[...].astype(BF16), w_ref[...], preferred_element_type=F32)
    c = _scan_add(_log_sigmoid(fl + bf_ref[...]), 0) + carry_ref[0:1, :]
    carry_ref[...] = jnp.broadcast_to(c[c.shape[0] - 1:, :], carry_ref.shape)
    o_ref[...] = c


def fox_gate(h, w_f, b_f):
    lp = h.shape[0]
    tm = OUT_TM
    return pl.pallas_call(
        _fox_gate_body,
        grid=(lp // tm,),
        in_specs=[pl.BlockSpec((tm, D_MODEL), lambda i: (i, 0)),
                  pl.BlockSpec((D_MODEL, LANE), lambda i: (0, 0)),
                  pl.BlockSpec((1, LANE), lambda i: (0, 0))],
        out_specs=pl.BlockSpec((tm, LANE), lambda i: (i, 0)),
        out_shape=jax.ShapeDtypeStruct((lp, LANE), F32),
        scratch_shapes=[pltpu.VMEM((SUBLANE, LANE), F32)],
        compiler_params=_cparams(("arbitrary",)),
        name="fox_gate",
    )(h, w_f, b_f)


def _fox_attn_body(q_ref, k_ref, v_ref, cq_ref, ck_ref, o_ref):
    i = pl.program_id(1)
    tq = q_ref.shape[0]
    scale = FOX_HEAD_DIM ** -0.5
    q = q_ref[...]
    cq = cq_ref[...]

    def step(j, carry, diagonal):
        m, l, acc = carry
        start = pl.multiple_of(j * tq, tq)
        kb = k_ref[pl.ds(start, tq), :]
        vb = v_ref[pl.ds(start, tq), :]
        s = lax.dot_general(q, kb, (_NT, ((), ())), preferred_element_type=F32) * scale
        s = s + cq - ck_ref[j]
        if diagonal:
            row = lax.broadcasted_iota(jnp.int32, s.shape, 0)
            col = lax.broadcasted_iota(jnp.int32, s.shape, 1)
            s = jnp.where(col <= row, s, -jnp.inf)
        m_new = jnp.maximum(m, jnp.max(s, -1, keepdims=True))
        p = jnp.exp(s - m_new)
        corr = jnp.exp(m - m_new)
        l = corr * l + jnp.sum(p, -1, keepdims=True)
        acc = corr * acc + jnp.dot(p.astype(BF16), vb, preferred_element_type=F32)
        return m_new, l, acc

    init = (jnp.full((tq, 1), -jnp.inf, F32), jnp.zeros((tq, 1), F32), jnp.zeros((tq, FOX_HEAD_DIM), F32))
    carry = lax.fori_loop(0, i, lambda j, cr: step(j, cr, False), init)
    _, l, acc = step(i, carry, True)
    o_ref[...] = (acc / l).astype(o_ref.dtype)


def fox_attn(qkv, c_col, c_row):
    lp = qkv.shape[0]
    tq = FOX_TQ
    nk = lp // tq
    hh = FOX_HEADS
    return pl.pallas_call(
        _fox_attn_body,
        grid=(hh, lp // tq),
        in_specs=[pl.BlockSpec((tq, FOX_HEAD_DIM), lambda h, i: (i, h)),
                  pl.BlockSpec((lp, FOX_HEAD_DIM), lambda h, i: (0, hh + h)),
                  pl.BlockSpec((lp, FOX_HEAD_DIM), lambda h, i: (0, 2 * hh + h)),
                  pl.BlockSpec((None, tq, 1), lambda h, i: (h, i, 0)),
                  pl.BlockSpec((None, nk, 1, tq), lambda h, i: (h, 0, 0, 0))],
        out_specs=pl.BlockSpec((tq, FOX_HEAD_DIM), lambda h, i: (i, h)),
        out_shape=jax.ShapeDtypeStruct((lp, D_MODEL), BF16),
        compiler_params=_cparams(("parallel", "arbitrary")),
        name="fox_attn",
    )(qkv, qkv, qkv, c_col, c_row)


def fox_params(w_in, b_f, w_o):
    return dict(w_qkv=w_in[:, :3 * D_MODEL].astype(BF16),
                w_f=_pad_cols(w_in[:, 3 * D_MODEL:], LANE).astype(BF16),
                b_f=_pad_cols(b_f.reshape(1, FOX_HEADS), LANE), w_o=w_o.astype(BF16))


def fox_mixer(h, p):
    lp = h.shape[0]
    qkv = proj(h, p["w_qkv"], 768, BF16)
    c = fox_gate(h, p["w_f"], p["b_f"])[:, :FOX_HEADS]
    c_t = c.T
    c_col = c_t.reshape(FOX_HEADS, lp, 1)
    c_row = c_t.reshape(FOX_HEADS, lp // FOX_TQ, 1, FOX_TQ)
    return fox_attn(qkv, c_col, c_row)


def _mlstm_body(x_ref, gc_ref, gr_ref, bc_ref, br_ref, ng_ref, o_ref, c_ref, n_ref, m_ref):
    cm = x_ref.shape[0]
    hh = ML_HEADS

    @pl.when(pl.program_id(0) == 0)
    def _():
        c_ref[...] = jnp.zeros_like(c_ref)
        n_ref[...] = jnp.zeros_like(n_ref)
        m_ref[...] = jnp.zeros_like(m_ref)

    gc = gc_ref[...] + bc_ref[...]
    bcum_c = _scan_add(_log_sigmoid(gc), 0)
    gr = gr_ref[...] + br_ref[:, 0:1]
    bcum_r = _scan_add(_log_sigmoid(gr), 1)
    row = lax.broadcasted_iota(jnp.int32, (cm, cm), 0)
    col = lax.broadcasted_iota(jnp.int32, (cm, cm), 1)
    causal = col <= row

    for h in range(hh):
        b_col = bcum_c[:, hh + h:hh + h + 1]
        ig_col = gc[:, h:h + 1]
        b_row = bcum_r[hh + h:hh + h + 1, :]
        ig_row = gr[h:h + 1, :]
        m_prev = m_ref[h][0:1, 0:1]
        q = x_ref[:, h * ML_DQK:(h + 1) * ML_DQK].astype(BF16)
        k = x_ref[:, ML_QK + h * ML_DQK:ML_QK + (h + 1) * ML_DQK] * (ML_DQK ** -0.5)
        v = x_ref[:, 2 * ML_QK + h * ML_DV:2 * ML_QK + (h + 1) * ML_DV].astype(BF16)
        og = x_ref[:, 2 * ML_QK + ML_V + h * ML_DV:2 * ML_QK + ML_V + (h + 1) * ML_DV]

        dmat = jnp.where(causal, b_col - b_row + ig_row, -jnp.inf)
        inter = b_col + m_prev
        m_t = jnp.maximum(inter, jnp.max(dmat, -1, keepdims=True))
        w_intra = jnp.exp(dmat - m_t)
        w_inter = jnp.exp(inter - m_t)
        s = lax.dot_general(q, k.astype(BF16), (_NT, ((), ())), preferred_element_type=F32) * w_intra
        c_st = c_ref[h]
        n_st = n_ref[h][0:1, :]
        num = (jnp.dot(s.astype(BF16), v, preferred_element_type=F32)
               + w_inter * jnp.dot(q, c_st.astype(BF16), preferred_element_type=F32))
        den = jnp.sum(s, -1, keepdims=True) + w_inter * jnp.sum(q.astype(F32) * n_st, -1, keepdims=True)
        hv = num / jnp.maximum(jnp.abs(den), jnp.exp(-m_t))

        b_end = b_col[cm - 1:cm, :]
        d_end = b_end - b_col + ig_col
        m_new = jnp.maximum(b_end + m_prev, jnp.max(d_end, 0, keepdims=True))
        kw = k * jnp.exp(d_end - m_new)
        carry = jnp.exp(b_end + m_prev - m_new)
        c_ref[h] = carry * c_st + lax.dot_general(kw.astype(BF16), v, (_TN, ((), ())), preferred_element_type=F32)
        n_ref[h] = jnp.broadcast_to(carry * n_st + jnp.sum(kw, 0, keepdims=True), n_ref.shape[1:])
        m_ref[h] = jnp.broadcast_to(m_new, m_ref.shape[1:])

        hn = hv * lax.rsqrt(jnp.mean(hv * hv, -1, keepdims=True) + ML_EPS)
        sl = slice(h * ML_DV, (h + 1) * ML_DV)
        o_ref[:, sl] = (hn * ng_ref[:, sl] * jax.nn.sigmoid(og)).astype(o_ref.dtype)


def mlstm_params(w_in, b_if, norm_g, w_o):
    nmain = 2 * ML_QK + 2 * ML_V
    b_c = _pad_cols(b_if.reshape(1, 2 * ML_HEADS), LANE)
    b_r = jnp.broadcast_to(b_if.reshape(2 * ML_HEADS, 1), (2 * ML_HEADS, LANE))
    return dict(w_main=w_in[:, :nmain].astype(BF16), w_g=_pad_cols(w_in[:, nmain:], LANE).astype(BF16),
                b_c=b_c, b_r=b_r, norm_g=norm_g.reshape(1, ML_V), w_o=w_o.astype(BF16))


def mlstm_mixer(h, p):
    lp = h.shape[0]
    cm = ML_CHUNK
    nmain = 2 * ML_QK + 2 * ML_V
    main = proj(h, p["w_main"], 768, F32)
    gates = proj(h, p["w_g"], LANE, F32)
    gates_r = gates[:, :2 * ML_HEADS].T
    return pl.pallas_call(
        _mlstm_body,
        grid=(lp // cm,),
        in_specs=[pl.BlockSpec((cm, nmain), lambda i: (i, 0)),
                  pl.BlockSpec((cm, LANE), lambda i: (i, 0)),
                  pl.BlockSpec((2 * ML_HEADS, cm), lambda i: (0, i)),
                  pl.BlockSpec((1, LANE), lambda i: (0, 0)),
                  pl.BlockSpec((2 * ML_HEADS, LANE), lambda i: (0, 0)),
                  pl.BlockSpec((1, ML_V), lambda i: (0, 0))],
        out_specs=pl.BlockSpec((cm, ML_V), lambda i: (i, 0)),
        out_shape=jax.ShapeDtypeStruct((lp, ML_V), BF16),
        scratch_shapes=[pltpu.VMEM((ML_HEADS, ML_DQK, ML_DV), F32),
                        pltpu.VMEM((ML_HEADS, SUBLANE, ML_DQK), F32),
                        pltpu.VMEM((ML_HEADS, SUBLANE, LANE), F32)],
        compiler_params=_cparams(("arbitrary",)),
        name="mlstm",
    )(main, gates, gates_r, p["b_c"], p["b_r"], p["norm_g"])


def kernel(x, meta_tokens, ln_g, ln_b, ffn_in, ffn_out, rwkv_mix, rwkv_w_rkv, rwkv_w0, rwkv_w1, rwkv_w2, rwkv_a0, rwkv_a1, rwkv_a2, rwkv_g1, rwkv_g2, rwkv_k_k, rwkv_k_a, rwkv_r_k, rwkv_gn_g, rwkv_gn_b, rwkv_w_o, rg_w_in, rg_conv_w, rg_conv_b, rg_w_a, rg_b_a, rg_w_x, rg_b_x, rg_lambda, rg_w_o, fox_w_in, fox_b_f, fox_w_o, ml_w_in, ml_b_if, ml_norm_g, ml_w_o):
    batch, seq, _ = x.shape
    assert batch == 1
    l = seq + N_META
    lp = -(-l // ROW_ALIGN) * ROW_ALIGN
    h = jnp.concatenate([meta_tokens.astype(x.dtype), x[0], jnp.zeros((lp - l, D_MODEL), x.dtype)], axis=0)

    ffn_in_b = ffn_in.astype(BF16)
    ffn_out_b = ffn_out.astype(BF16)
    rw = rwkv_params(rwkv_mix, rwkv_w_rkv, rwkv_w0, rwkv_w1, rwkv_w2, rwkv_a0, rwkv_a1, rwkv_a2, rwkv_g1,
                     rwkv_g2, rwkv_k_k, rwkv_k_a, rwkv_r_k, rwkv_gn_g, rwkv_gn_b, rwkv_w_o)
    rg = rg_params(rg_w_in, rg_conv_w, rg_conv_b, rg_w_a, rg_b_a, rg_w_x, rg_b_x, rg_lambda, rg_w_o)
    fx = fox_params(fox_w_in, fox_b_f, fox_w_o)
    ml = mlstm_params(ml_w_in, ml_b_if, ml_norm_g, ml_w_o)
    mixers = ((rwkv_mixer, rw), (rg_mixer, rg), (fox_mixer, fx), (mlstm_mixer, ml))

    vec = lambda t: t.reshape(1, D_MODEL)
    for layer in range(DEPTH):
        h = ffn_ln(h, ffn_in_b, ffn_out_b, layer, 0, vec(ln_g[layer, 0]), vec(ln_b[layer, 0]))
        mixer, p = mixers[layer % len(mixers)]
        y = mixer(h, p)
        h = outproj_ln(h, y, p["w_o"], vec(ln_g[layer, 1]), vec(ln_b[layer, 1]))
        h = ffn_ln(h, ffn_in_b, ffn_out_b, layer, 1, vec(ln_g[layer, 2]), vec(ln_b[layer, 2]))
    return h[N_META:l][None]
```

```python
import functools

import numpy as np
import jax
import jax.numpy as jnp
from jax import lax
from jax.experimental import pallas as pl
from jax.experimental.pallas import tpu as pltpu

F32 = jnp.float32
BF16 = jnp.bfloat16

D_MODEL = 2048
DEPTH = 4
N_META = 16
D_FF = 5632
ALPHA = (2 * DEPTH) ** 0.25
LN_EPS = 1e-5
RWKV_HEAD_DIM = 64
RWKV_HEADS = D_MODEL // RWKV_HEAD_DIM
RWKV_GN_EPS = 64e-5
RG_BLOCKS = 16
D_RNN = 2688
RG_BLOCK_DIM = D_RNN // RG_BLOCKS
RG_C = 8.0
CONV_W = 4
FOX_HEAD_DIM = 128
FOX_HEADS = D_MODEL // FOX_HEAD_DIM
ML_HEADS = 8
ML_QK = D_MODEL // 2
ML_V = D_MODEL
ML_DQK = ML_QK // ML_HEADS
ML_DV = ML_V // ML_HEADS
ML_EPS = 1e-6

LANE = 128
SUBLANE = 8
ROW_ALIGN = 1280
VMEM_LIMIT = 56 * 1024 * 1024

FFN_TM, FFN_TN = 640, 512
PROJ_TM = 640
OUT_TM = 256
RWKV_PREP_TM, RWKV_PREP_TN = 256, 512
RWKV_CHUNK = 64
RWKV_HEAD_GROUP = 32
RG_TM = 256
RG_KWIN = 512
FOX_TQ = 640
FOX_STRIP = 16
ML_CHUNK = 256

HI = lax.Precision.HIGHEST
LOG2E = float(np.log2(np.e))


def _cparams(sem):
    return pltpu.CompilerParams(dimension_semantics=sem, vmem_limit_bytes=VMEM_LIMIT)


def _ln(y, g, b):
    mu = jnp.mean(y, -1, keepdims=True)
    yc = y - mu
    var = jnp.mean(yc * yc, -1, keepdims=True)
    return yc * lax.rsqrt(var + LN_EPS) * g + b


def _softplus(z):
    return jnp.maximum(z, 0.0) + jnp.log1p(jnp.exp(-jnp.abs(z)))


def _log_sigmoid(z):
    return -_softplus(-z)


def _scan_add(x, axis):
    n = x.shape[axis]
    idx = lax.broadcasted_iota(jnp.int32, x.shape, axis)
    s = 1
    while s < n:
        x = x + jnp.where(idx >= s, pltpu.roll(x, s, axis), 0.0)
        s *= 2
    return x


def _ffn_ln_body(x_ref, wg_ref, wu_ref, wo_ref, g_ref, b_ref, o_ref, xb_ref, acc_ref):
    j = pl.program_id(1)

    @pl.when(j == 0)
    def _():
        xb_ref[...] = x_ref[...].astype(BF16)
        acc_ref[...] = jnp.zeros_like(acc_ref)

    xb = xb_ref[...]
    hg = jnp.dot(xb, wg_ref[...], preferred_element_type=F32)
    hu = jnp.dot(xb, wu_ref[...], preferred_element_type=F32)
    act = (hg * jax.nn.sigmoid(hg) * hu).astype(BF16)
    acc_ref[...] += jnp.dot(act, wo_ref[...], preferred_element_type=F32)

    @pl.when(j == pl.num_programs(1) - 1)
    def _():
        y = ALPHA * x_ref[...] + 0.5 * acc_ref[...]
        o_ref[...] = _ln(y, g_ref[...], b_ref[...])


def ffn_ln(h, w_in, w_out, layer, idx, g, b):
    lp = h.shape[0]
    tm, tn = FFN_TM, FFN_TN
    nf = D_FF // tn
    return pl.pallas_call(
        _ffn_ln_body,
        grid=(lp // tm, nf),
        in_specs=[
            pl.BlockSpec((tm, D_MODEL), lambda i, j: (i, 0)),
            pl.BlockSpec((None, None, D_MODEL, tn), lambda i, j: (layer, idx, 0, j)),
            pl.BlockSpec((None, None, D_MODEL, tn), lambda i, j: (layer, idx, 0, j + nf)),
            pl.BlockSpec((None, None, tn, D_MODEL), lambda i, j: (layer, idx, j, 0)),
            pl.BlockSpec((1, D_MODEL), lambda i, j: (0, 0)),
            pl.BlockSpec((1, D_MODEL), lambda i, j: (0, 0)),
        ],
        out_specs=pl.BlockSpec((tm, D_MODEL), lambda i, j: (i, 0)),
        out_shape=jax.ShapeDtypeStruct((lp, D_MODEL), F32),
        scratch_shapes=[pltpu.VMEM((tm, D_MODEL), BF16), pltpu.VMEM((tm, D_MODEL), F32)],
        compiler_params=_cparams(("parallel", "arbitrary")),
        name="ffn_ln",
    )(h, w_in, w_in, w_out, g, b)


def _proj_body(x_ref, w_ref, sc_ref, o_ref, xb_ref):
    @pl.when(pl.program_id(1) == 0)
    def _():
        xb_ref[...] = x_ref[...].astype(BF16)

    acc = jnp.dot(xb_ref[...], w_ref[...], preferred_element_type=F32)
    o_ref[...] = (acc * sc_ref[...]).astype(o_ref.dtype)


def proj(h, w, col_scale, tn, out_dtype):
    lp, n = h.shape[0], w.shape[1]
    tm = PROJ_TM
    return pl.pallas_call(
        _proj_body,
        grid=(lp // tm, n // tn),
        in_specs=[pl.BlockSpec((tm, D_MODEL), lambda i, j: (i, 0)),
                  pl.BlockSpec((D_MODEL, tn), lambda i, j: (0, j)),
                  pl.BlockSpec((1, tn), lambda i, j: (0, j))],
        out_specs=pl.BlockSpec((tm, tn), lambda i, j: (i, j)),
        out_shape=jax.ShapeDtypeStruct((lp, n), out_dtype),
        scratch_shapes=[pltpu.VMEM((tm, D_MODEL), BF16)],
        compiler_params=_cparams(("parallel", "arbitrary")),
        name="proj",
    )(h, w, col_scale)


def _outproj_ln_body(x_ref, y_ref, w_ref, g_ref, b_ref, o_ref):
    acc = jnp.dot(y_ref[...], w_ref[...], preferred_element_type=F32)
    o_ref[...] = _ln(ALPHA * x_ref[...] + acc, g_ref[...], b_ref[...])


def outproj_ln(h, y, w, g, b):
    lp, k = y.shape
    tm = OUT_TM
    return pl.pallas_call(
        _outproj_ln_body,
        grid=(lp // tm,),
        in_specs=[pl.BlockSpec((tm, D_MODEL), lambda i: (i, 0)),
                  pl.BlockSpec((tm, k), lambda i: (i, 0)),
                  pl.BlockSpec((k, D_MODEL), lambda i: (0, 0)),
                  pl.BlockSpec((1, D_MODEL), lambda i: (0, 0)),
                  pl.BlockSpec((1, D_MODEL), lambda i: (0, 0))],
        out_specs=pl.BlockSpec((tm, D_MODEL), lambda i: (i, 0)),
        out_shape=jax.ShapeDtypeStruct((lp, D_MODEL), F32),
        compiler_params=_cparams(("parallel",)),
        name="outproj_ln",
    )(h, y, w, g, b)


def _rwkv_prep_body(x_ref, xp_ref, mix_ref, wrkv_ref, w1_ref, a1_ref, g1_ref, w2_ref, a2_ref, g2_ref,
                    w0_ref, a0_ref, kk_ref, ka_ref, bd_ref,
                    r_o, lw_o, k_o, v_o, kk_o, b_o, g_o,
                    xe_ref, xs_ref, tw_ref, ta_ref, tg_ref):
    i, j = pl.program_id(0), pl.program_id(1)
    tm = x_ref.shape[0]

    @pl.when(j == 0)
    def _():
        x = x_ref[...]
        xe_ref[0:SUBLANE, :] = jnp.where(i == 0, 0.0, xp_ref[...])
        xe_ref[SUBLANE:, :] = x
        xx = xe_ref[pl.ds(SUBLANE - 1, tm), :] - x
        mix = mix_ref[...]
        for s in range(3):
            xs_ref[s] = (x + xx * mix[s:s + 1, :]).astype(BF16)
        xs_w = (x + xx * mix[3:4, :]).astype(BF16)
        xs_a = (x + xx * mix[4:5, :]).astype(BF16)
        xs_g = (x + xx * mix[5:6, :]).astype(BF16)
        tw_ref[...] = jnp.tanh(jnp.dot(xs_w, w1_ref[...], preferred_element_type=F32)).astype(BF16)
        ta_ref[...] = jnp.dot(xs_a, a1_ref[...], preferred_element_type=F32).astype(BF16)
        tg_ref[...] = jax.nn.sigmoid(jnp.dot(xs_g, g1_ref[...], preferred_element_type=F32)).astype(BF16)

    r = jnp.dot(xs_ref[0], wrkv_ref[0], preferred_element_type=F32)
    k = jnp.dot(xs_ref[1], wrkv_ref[1], preferred_element_type=F32)
    v = jnp.dot(xs_ref[2], wrkv_ref[2], preferred_element_type=F32)
    wz = w0_ref[...] + jnp.dot(tw_ref[...], w2_ref[...], preferred_element_type=F32)
    az = a0_ref[...] + jnp.dot(ta_ref[...], a2_ref[...], preferred_element_type=F32)
    g = jnp.dot(tg_ref[...], g2_ref[...], preferred_element_type=F32)

    w_log = -_softplus(-wz) - 0.5
    a = jax.nn.sigmoid(az)
    kk = k * kk_ref[...]
    kk2 = kk * kk
    bd = bd_ref[...]
    ss = jnp.concatenate(
        [jnp.dot(kk2[:, c * LANE:(c + 1) * LANE], bd, precision=HI, preferred_element_type=F32)
         for c in range(kk2.shape[1] // LANE)], axis=1)
    kkn = kk * lax.rsqrt(ss + 1e-12)

    r_o[...] = r
    lw_o[...] = -jnp.exp(w_log)
    k_o[...] = k * (1.0 + (a - 1.0) * ka_ref[...])
    v_o[...] = v
    kk_o[...] = kkn
    b_o[...] = kkn * a
    g_o[...] = g


def rwkv_prep(h, p):
    lp = h.shape[0]
    tm, tn = RWKV_PREP_TM, RWKV_PREP_TN
    row = lambda i, j: (i, 0)
    col = lambda i, j: (0, j)
    full = lambda i, j: (0, 0)
    lw, la, lg = p["w1"].shape[1], p["a1"].shape[1], p["g1"].shape[1]
    out = jax.ShapeDtypeStruct((lp, D_MODEL), F32)
    return pl.pallas_call(
        _rwkv_prep_body,
        grid=(lp // tm, D_MODEL // tn),
        in_specs=[
            pl.BlockSpec((tm, D_MODEL), row),
            pl.BlockSpec((SUBLANE, D_MODEL), lambda i, j: (jnp.maximum(i * (tm // SUBLANE) - 1, 0), 0)),
            pl.BlockSpec((SUBLANE, D_MODEL), full),
            pl.BlockSpec((3, D_MODEL, tn), lambda i, j: (0, 0, j)),
            pl.BlockSpec((D_MODEL, lw), full),
            pl.BlockSpec((D_MODEL, la), full),
            pl.BlockSpec((D_MODEL, lg), full),
            pl.BlockSpec((lw, tn), col),
            pl.BlockSpec((la, tn), col),
            pl.BlockSpec((lg, tn), col),
            pl.BlockSpec((1, tn), col),
            pl.BlockSpec((1, tn), col),
            pl.BlockSpec((1, tn), col),
            pl.BlockSpec((1, tn), col),
            pl.BlockSpec((LANE, LANE), full),
        ],
        out_specs=[pl.BlockSpec((tm, tn), lambda i, j: (i, j))] * 7,
        out_shape=[out] * 7,
        scratch_shapes=[pltpu.VMEM((tm + SUBLANE, D_MODEL), F32),
                        pltpu.VMEM((3, tm, D_MODEL), BF16),
                        pltpu.VMEM((tm, lw), BF16), pltpu.VMEM((tm, la), BF16), pltpu.VMEM((tm, lg), BF16)],
        compiler_params=_cparams(("parallel", "arbitrary")),
        name="rwkv_prep",
    )(h, h, p["mix"], p["w_rkv"], p["w1"], p["a1"], p["g1"], p["w2"], p["a2"], p["g2"],
      p["w0"], p["a0"], p["k_k"], p["k_a"], p["bd"])


RWKV_MXU_BF16 = True


def _rw_mm(a, b, dims):
    if RWKV_MXU_BF16:
        return lax.dot_general(a.astype(BF16), b.astype(BF16), (dims, ((), ())), preferred_element_type=F32)
    return lax.dot_general(a, b, (dims, ((), ())), precision=HI, preferred_element_type=F32)


_NN = ((1,), (0,))
_NT = ((1,), (1,))
_TN = ((0,), (0,))


def _rwkv_chunk_body(r_ref, lw_ref, k_ref, v_ref, kk_ref, b_ref, g_ref, rk_ref, gng_ref, gnb_ref, tri_ref,
                     o_ref, s_ref):
    c = r_ref.shape[0]
    n = RWKV_HEAD_DIM

    @pl.when(pl.program_id(0) == 0)
    def _():
        s_ref[...] = jnp.zeros_like(s_ref)

    lw = lw_ref[...]
    cum = jnp.dot(tri_ref[...], lw, precision=HI, preferred_element_type=F32)
    e_pos = jnp.exp(cum)
    e_neg = jnp.exp(-cum)
    e_end = e_pos[c - 1:c, :]
    at_all = -kk_ref[...] * jnp.exp(cum - lw)
    rt_all = r_ref[...] * e_pos
    bb_all = b_ref[...] * e_neg
    kb_all = k_ref[...] * e_neg

    row = lax.broadcasted_iota(jnp.int32, (c, c), 0)
    colm = lax.broadcasted_iota(jnp.int32, (c, c), 1)
    strict = colm < row
    incl = colm <= row

    for g0 in range(0, RWKV_HEADS, RWKV_HEAD_GROUP):
        hs = range(g0, g0 + RWKV_HEAD_GROUP)
        sls = [slice(h * n, (h + 1) * n) for h in hs]
        v_h = [v_ref[:, sl] for sl in sls]
        lhs = [jnp.concatenate([at_all[:, sl], rt_all[:, sl]], axis=0) for sl in sls]
        rhs = [jnp.concatenate([bb_all[:, sl], kb_all[:, sl]], axis=0) for sl in sls]
        gram = [_rw_mm(a, b_, _NT) for a, b_ in zip(lhs, rhs)]
        s0 = [s_ref[h] for h in hs]
        z = [_rw_mm(a, s_, _NT) for a, s_ in zip(lhs, s0)]
        a_ab = [jnp.where(strict, gm[:c, :c], 0.0) for gm in gram]
        a_ak = [jnp.where(strict, gm[:c, c:], 0.0) for gm in gram]
        a_r = [jnp.concatenate([jnp.where(incl, gm[c:, :c], 0.0), jnp.where(incl, gm[c:, c:], 0.0)], axis=1)
               for gm in gram]
        u = [z_[:c] + _rw_mm(a, v_, _NN) for z_, a, v_ in zip(z, a_ak, v_h)]
        pw = a_ab
        span = 1
        while span < c:
            u = [u_ + _rw_mm(p_, u_, _NN) for u_, p_ in zip(u, pw)]
            span *= 2
            if span < c:
                pw = [_rw_mm(p_, p_, _NN) for p_ in pw]
        uv = [jnp.concatenate([u_, v_], axis=0) for u_, v_ in zip(u, v_h)]
        y = [z_[c:] + _rw_mm(a, uv_, _NN) for z_, a, uv_ in zip(z, a_r, uv)]
        for h, sl, s_, uv_, rhs_ in zip(hs, sls, s0, uv, rhs):
            e_end_h = e_end[:, sl]
            s_ref[h] = s_ * e_end_h + _rw_mm(uv_, rhs_ * e_end_h, _TN)
        for sl, y_, v_ in zip(sls, y, v_h):
            mu = jnp.mean(y_, -1, keepdims=True)
            yc = y_ - mu
            var = jnp.mean(yc * yc, -1, keepdims=True)
            yn = yc * lax.rsqrt(var + RWKV_GN_EPS) * gng_ref[:, sl] + gnb_ref[:, sl]
            bonus = jnp.sum(r_ref[:, sl] * k_ref[:, sl] * rk_ref[:, sl], -1, keepdims=True) * v_
            o_ref[:, sl] = ((yn + bonus) * g_ref[:, sl]).astype(o_ref.dtype)


def rwkv_chunk(r, lw, k, v, kk, b, g, r_k, gn_g, gn_b):
    lp = r.shape[0]
    c = RWKV_CHUNK
    tri = jnp.asarray(np.tril(np.ones((c, c), np.float32)))
    blk = pl.BlockSpec((c, D_MODEL), lambda i: (i, 0))
    vec = pl.BlockSpec((1, D_MODEL), lambda i: (0, 0))
    return pl.pallas_call(
        _rwkv_chunk_body,
        grid=(lp // c,),
        in_specs=[blk] * 7 + [vec] * 3 + [pl.BlockSpec((c, c), lambda i: (0, 0))],
        out_specs=blk,
        out_shape=jax.ShapeDtypeStruct((lp, D_MODEL), BF16),
        scratch_shapes=[pltpu.VMEM((RWKV_HEADS, RWKV_HEAD_DIM, RWKV_HEAD_DIM), F32)],
        compiler_params=_cparams(("arbitrary",)),
        name="rwkv_chunk",
    )(r, lw, k, v, kk, b, g, r_k, gn_g, gn_b, tri)


def _pad_rows(a, rows):
    return jnp.pad(a, ((0, rows - a.shape[0]), (0, 0)))


def _pad_cols(a, cols):
    return jnp.pad(a, ((0, 0), (0, cols - a.shape[1])))


def rwkv_params(mix, w_rkv, w0, w1, w2, a0, a1, a2, g1, g2, k_k, k_a, r_k, gn_g, gn_b, w_o):
    lora = LANE
    bd = np.kron(np.eye(LANE // RWKV_HEAD_DIM, dtype=np.float32),
                 np.ones((RWKV_HEAD_DIM, RWKV_HEAD_DIM), np.float32))
    row = lambda t: t.reshape(1, D_MODEL)
    return dict(
        mix=_pad_rows(mix, SUBLANE), w_rkv=w_rkv.astype(BF16),
        w1=_pad_cols(w1, lora).astype(BF16), w2=_pad_rows(w2, lora).astype(BF16),
        a1=_pad_cols(a1, lora).astype(BF16), a2=_pad_rows(a2, lora).astype(BF16),
        g1=g1.astype(BF16), g2=g2.astype(BF16),
        w0=row(w0), a0=row(a0), k_k=row(k_k), k_a=row(k_a), bd=jnp.asarray(bd),
        r_k=row(r_k), gn_g=row(gn_g), gn_b=row(gn_b), w_o=w_o.astype(BF16))


def rwkv_mixer(h, p):
    r, lw, k, v, kk, b, g = rwkv_prep(h, p)
    return rwkv_chunk(r, lw, k, v, kk, b, g, p["r_k"], p["gn_g"], p["gn_b"])


def _rg_windows():
    starts = []
    for c in range(D_RNN // LANE):
        first_block = (c * LANE) // RG_BLOCK_DIM
        ks = (first_block * RG_BLOCK_DIM) // LANE * LANE
        starts.append(min(ks, D_RNN - RG_KWIN))
    return starts


def _rg_compact(w):
    dense = jnp.zeros((D_RNN, D_RNN), F32)
    for nblk in range(RG_BLOCKS):
        s = nblk * RG_BLOCK_DIM
        dense = lax.dynamic_update_slice(dense, w[nblk], (s, s))
    tiles = [dense[ks:ks + RG_KWIN, c * LANE:(c + 1) * LANE] for c, ks in enumerate(_rg_windows())]
    return jnp.stack(tiles).astype(BF16)


def _gelu_tanh(x):
    return 0.5 * x * (1.0 + jnp.tanh(np.sqrt(2.0 / np.pi).astype(np.float32) * (x + 0.044715 * (x * x * x))))


def _rg_body(gate_ref, u_ref, up_ref, cw_ref, cb_ref, wa_ref, wx_ref, ba_ref, bx_ref, lam_ref, o_ref,
             ue_ref, carry_ref):
    i = pl.program_id(0)
    tm = u_ref.shape[0]

    @pl.when(i == 0)
    def _():
        carry_ref[...] = jnp.zeros_like(carry_ref)

    u = u_ref[...]
    ue_ref[0:SUBLANE, :] = jnp.where(i == 0, 0.0, up_ref[...])
    ue_ref[SUBLANE:, :] = u
    cw = cw_ref[...]
    uc = cb_ref[...] + u * cw[CONV_W - 1:CONV_W, :]
    for d in range(1, CONV_W):
        uc = uc + ue_ref[pl.ds(SUBLANE - d, tm), :] * cw[CONV_W - 1 - d:CONV_W - d, :]
    ucb = uc.astype(BF16)
    za, zx = [], []
    for c, ks in enumerate(_rg_windows()):
        win = ucb[:, ks:ks + RG_KWIN]
        za.append(jnp.dot(win, wa_ref[c], preferred_element_type=F32))
        zx.append(jnp.dot(win, wx_ref[c], preferred_element_type=F32))
    r = jax.nn.sigmoid(jnp.concatenate(za, axis=1) + ba_ref[...])
    ig = jax.nn.sigmoid(jnp.concatenate(zx, axis=1) + bx_ref[...])
    log_a = -RG_C * r * _softplus(-lam_ref[...])
    a = jnp.exp(log_a)
    th = jnp.tanh(log_a)
    b = jnp.sqrt(-2.0 * th / (1.0 - th)) * (ig * uc)
    rowi = lax.broadcasted_iota(jnp.int32, a.shape, 0)
    s = 1
    while s < tm:
        keep = rowi >= s
        b = jnp.where(keep, a * pltpu.roll(b, s, 0) + b, b)
        a = jnp.where(keep, a * pltpu.roll(a, s, 0), a)
        s *= 2
    hs = a * carry_ref[0:1, :] + b
    carry_ref[...] = jnp.broadcast_to(hs[tm - 1:tm, :], carry_ref.shape)
    o_ref[...] = (_gelu_tanh(gate_ref[...]) * hs).astype(o_ref.dtype)


def rg_params(w_in, conv_w, conv_b, w_a, b_a, w_x, b_x, lam, w_o):
    row = lambda t: t.reshape(1, D_RNN)
    return dict(w_in=w_in.astype(BF16), conv_w=_pad_rows(conv_w, SUBLANE), conv_b=row(conv_b),
                w_a=_rg_compact(w_a), w_x=_rg_compact(w_x), b_a=row(b_a), b_x=row(b_x),
                lam=row(lam), w_o=w_o.astype(BF16), ones=jnp.ones((1, 2 * D_RNN), F32))


def rg_mixer(h, p):
    lp = h.shape[0]
    tm = RG_TM
    gu = proj(h, p["w_in"], p["ones"], 768, F32)
    ntile = D_RNN // LANE
    vec = pl.BlockSpec((1, D_RNN), lambda i: (0, 0))
    wspec = pl.BlockSpec((ntile, RG_KWIN, LANE), lambda i: (0, 0, 0))
    return pl.pallas_call(
        _rg_body,
        grid=(lp // tm,),
        in_specs=[pl.BlockSpec((tm, D_RNN), lambda i: (i, 0)),
                  pl.BlockSpec((tm, D_RNN), lambda i: (i, 1)),
                  pl.BlockSpec((SUBLANE, D_RNN), lambda i: (jnp.maximum(i * (tm // SUBLANE) - 1, 0), 1)),
                  pl.BlockSpec((SUBLANE, D_RNN), lambda i: (0, 0)),
                  vec, wspec, wspec, vec, vec, vec],
        out_specs=pl.BlockSpec((tm, D_RNN), lambda i: (i, 0)),
        out_shape=jax.ShapeDtypeStruct((lp, D_RNN), BF16),
        scratch_shapes=[pltpu.VMEM((tm + SUBLANE, D_RNN), F32), pltpu.VMEM((SUBLANE, D_RNN), F32)],
        compiler_params=_cparams(("arbitrary",)),
        name="rg_lru",
    )(gu, gu, gu, p["conv_w"], p["conv_b"], p["w_a"], p["w_x"], p["b_a"], p["b_x"], p["lam"])


def _split3(x):
    hi = x.astype(BF16).astype(F32)
    mid = (x - hi).astype(BF16).astype(F32)
    lo = (x - hi - mid).astype(BF16).astype(F32)
    return hi, mid, lo


def _fox_gate_body(x_ref, w_ref, bf_ref, cc_ref, qx_ref, kx_ref, carry_ref):
    @pl.when(pl.program_id(0) == 0)
    def _():
        carry_ref[...] = jnp.zeros_like(carry_ref)

    tm = x_ref.shape[0]
    fl = jnp.dot(x_ref[...].astype(BF16), w_ref[...], preferred_element_type=F32)
    c = _scan_add(_log_sigmoid(fl + bf_ref[...]), 0) + carry_ref[0:1, :]
    carry_ref[...] = jnp.broadcast_to(c[tm - 1:, :], carry_ref.shape)
    c = c * LOG2E
    cc_ref[...] = jnp.broadcast_to(c[0:1, :], cc_ref.shape)
    fine = c - c[0:1, :]
    lane = lax.broadcasted_iota(jnp.int32, (tm, LANE), 1)
    ones_mid = jnp.where(lane < 6, 1.0, 0.0)
    for h in range(FOX_HEADS):
        hi, mid, lo = [jnp.broadcast_to(t, (tm, LANE)) for t in _split3(fine[:, h:h + 1])]
        qx = jnp.where(lane == 0, hi, jnp.where(lane == 1, mid, lo))
        kx = -jnp.where(lane == 3, hi, jnp.where(lane == 4, mid, lo))
        qx_ref[:, h * LANE:(h + 1) * LANE] = jnp.where(lane < 3, qx, ones_mid).astype(BF16)
        kx_ref[:, h * LANE:(h + 1) * LANE] = jnp.where(lane < 3, 1.0, jnp.where(lane < 6, kx, 0.0)).astype(BF16)


def fox_gate(h, w_f, b_f):
    lp = h.shape[0]
    tm = FOX_TQ
    nt = lp // tm
    wide = jax.ShapeDtypeStruct((lp, FOX_HEADS * LANE), BF16)
    return pl.pallas_call(
        _fox_gate_body,
        grid=(nt,),
        in_specs=[pl.BlockSpec((tm, D_MODEL), lambda i: (i, 0)),
                  pl.BlockSpec((D_MODEL, LANE), lambda i: (0, 0)),
                  pl.BlockSpec((1, LANE), lambda i: (0, 0))],
        out_specs=[pl.BlockSpec((SUBLANE, LANE), lambda i: (i, 0)),
                   pl.BlockSpec((tm, FOX_HEADS * LANE), lambda i: (i, 0)),
                   pl.BlockSpec((tm, FOX_HEADS * LANE), lambda i: (i, 0))],
        out_shape=[jax.ShapeDtypeStruct((nt * SUBLANE, LANE), F32), wide, wide],
        scratch_shapes=[pltpu.VMEM((SUBLANE, LANE), F32)],
        compiler_params=_cparams(("arbitrary",)),
        name="fox_gate",
    )(h, w_f, b_f)


def _fox_attn_body(cc_ref, q_ref, qx_ref, k_ref, kx_ref, v_ref, o_ref,
                   s_ref, p_ref, mx_ref, d_ref, m_ref, corr_ref, acc_ref):
    h, i = pl.program_id(0), pl.program_id(1)
    tq = q_ref.shape[0]
    nl = tq // LANE
    q_aug = jnp.concatenate([q_ref[...], qx_ref[...]], axis=1)
    ones_col = (lax.broadcasted_iota(jnp.int32, (tq, LANE), 1) == 0).astype(BF16)
    strips = list(range(0, tq, FOX_STRIP))

    def tile_rows(j):
        return pl.ds(pl.multiple_of(j * tq, tq), tq)

    def logits(j, slot):
        rows = tile_rows(j)
        k_aug = jnp.concatenate([k_ref[rows, :], kx_ref[rows, :]], axis=1)
        s_ref[slot] = lax.dot_general(q_aug, k_aug, (_NT, ((), ())), preferred_element_type=F32)

    def softmax(j, slot, diagonal):
        delta = cc_ref[h, i] - cc_ref[h, j]
        for r0 in strips:
            rows = pl.ds(r0, FOX_STRIP)
            t = s_ref[slot, rows, :]
            if diagonal:
                row = r0 + lax.broadcasted_iota(jnp.int32, t.shape, 0)
                col = lax.broadcasted_iota(jnp.int32, t.shape, 1)
                t = jnp.where(col <= row, t, -jnp.inf)
                s_ref[slot, rows, :] = t
            mx_ref[rows, :] = functools.reduce(jnp.maximum, [t[:, c * LANE:(c + 1) * LANE] for c in range(nl)])
        m_old = m_ref[...]
        m_new = jnp.maximum(m_old, jnp.max(mx_ref[...], -1, keepdims=True) + delta)
        m_ref[...] = m_new
        corr_ref[slot] = jnp.exp2(m_old - m_new)
        d_ref[...] = jnp.broadcast_to(delta - m_new, d_ref.shape)
        for r0 in strips:
            rows = pl.ds(r0, FOX_STRIP)
            e = s_ref[slot, rows, :] + jnp.tile(d_ref[rows, :], (1, nl))
            p_ref[slot, rows, :] = jnp.exp2(e).astype(BF16)

    def accumulate(j, slot):
        vb = jnp.concatenate([v_ref[tile_rows(j), :], ones_col], axis=1)
        acc_ref[...] = corr_ref[slot] * acc_ref[...] + jnp.dot(p_ref[slot], vb, preferred_element_type=F32)

    m_ref[...] = jnp.full(m_ref.shape, -jnp.inf, F32)
    acc_ref[...] = jnp.zeros_like(acc_ref)
    p_ref[1] = jnp.zeros(p_ref.shape[1:], BF16)
    corr_ref[1] = jnp.ones(corr_ref.shape[1:], F32)
    logits(0, 0)

    def pair(a, carry):
        j = 2 * a
        accumulate(jnp.maximum(j - 1, 0), 1)
        softmax(j, 0, False)
        logits(j + 1, 1)
        accumulate(j, 0)
        softmax(j + 1, 1, False)
        logits(j + 2, 0)
        return carry

    lax.fori_loop(0, i // 2, pair, 0)

    @pl.when(i % 2 == 0)
    def _():
        accumulate(jnp.maximum(i - 1, 0), 1)
        softmax(i, 0, True)
        accumulate(i, 0)

    @pl.when(i % 2 == 1)
    def _():
        accumulate(jnp.maximum(i - 2, 0), 1)
        softmax(i - 1, 0, False)
        logits(i, 1)
        accumulate(i - 1, 0)
        softmax(i, 1, True)
        accumulate(i, 1)

    acc = acc_ref[...]
    o_ref[...] = (acc[:, :FOX_HEAD_DIM] / acc[:, FOX_HEAD_DIM:FOX_HEAD_DIM + 1]).astype(o_ref.dtype)


def fox_attn(qkv, qx, kx, cc):
    lp = qkv.shape[0]
    tq = FOX_TQ
    hh = FOX_HEADS
    hd = FOX_HEAD_DIM
    return pl.pallas_call(
        _fox_attn_body,
        grid=(hh, lp // tq),
        in_specs=[pl.BlockSpec(memory_space=pltpu.SMEM),
                  pl.BlockSpec((tq, hd), lambda h, i: (i, h)),
                  pl.BlockSpec((tq, LANE), lambda h, i: (i, h)),
                  pl.BlockSpec((lp, hd), lambda h, i: (0, hh + h)),
                  pl.BlockSpec((lp, LANE), lambda h, i: (0, h)),
                  pl.BlockSpec((lp, hd), lambda h, i: (0, 2 * hh + h))],
        out_specs=pl.BlockSpec((tq, hd), lambda h, i: (i, h)),
        out_shape=jax.ShapeDtypeStruct((lp, D_MODEL), BF16),
        scratch_shapes=[pltpu.VMEM((2, tq, tq), F32), pltpu.VMEM((2, tq, tq), BF16), pltpu.VMEM((tq, LANE), F32),
                        pltpu.VMEM((tq, LANE), F32), pltpu.VMEM((tq, 1), F32), pltpu.VMEM((2, tq, 1), F32),
                        pltpu.VMEM((tq, hd + LANE), F32)],
        compiler_params=_cparams(("parallel", "arbitrary")),
        name="fox_attn",
    )(cc, qkv, qx, qkv, kx, qkv)


def fox_params(w_in, b_f, w_o):
    q_scale = jnp.full((1, D_MODEL), LOG2E * FOX_HEAD_DIM ** -0.5, F32)
    return dict(w_qkv=w_in[:, :3 * D_MODEL].astype(BF16),
                qkv_scale=jnp.concatenate([q_scale, jnp.ones((1, 2 * D_MODEL), F32)], axis=1),
                w_f=_pad_cols(w_in[:, 3 * D_MODEL:], LANE).astype(BF16),
                b_f=_pad_cols(b_f.reshape(1, FOX_HEADS), LANE), w_o=w_o.astype(BF16))


def fox_mixer(h, p):
    qkv = proj(h, p["w_qkv"], p["qkv_scale"], 768, BF16)
    cc, qx, kx = fox_gate(h, p["w_f"], p["b_f"])
    cc = cc[::SUBLANE, :FOX_HEADS].T
    return fox_attn(qkv, qx, kx, cc)


def _mlstm_body(x_ref, gc_ref, gr_ref, bc_ref, br_ref, ng_ref, o_ref, c_ref, n_ref, m_ref):
    cm = x_ref.shape[0]
    hh = ML_HEADS

    @pl.when(pl.program_id(0) == 0)
    def _():
        c_ref[...] = jnp.zeros_like(c_ref)
        n_ref[...] = jnp.zeros_like(n_ref)
        m_ref[...] = jnp.zeros_like(m_ref)

    gc = gc_ref[...] + bc_ref[...]
    bcum_c = _scan_add(_log_sigmoid(gc), 0)
    gr = gr_ref[...] + br_ref[:, 0:1]
    bcum_r = _scan_add(_log_sigmoid(gr), 1)
    row = lax.broadcasted_iota(jnp.int32, (cm, cm), 0)
    col = lax.broadcasted_iota(jnp.int32, (cm, cm), 1)
    causal = col <= row

    for h in range(hh):
        b_col = bcum_c[:, hh + h:hh + h + 1]
        ig_col = gc[:, h:h + 1]
        b_row = bcum_r[hh + h:hh + h + 1, :]
        ig_row = gr[h:h + 1, :]
        m_prev = m_ref[h][0:1, 0:1]
        q = x_ref[:, h * ML_DQK:(h + 1) * ML_DQK].astype(BF16)
        k = x_ref[:, ML_QK + h * ML_DQK:ML_QK + (h + 1) * ML_DQK] * (ML_DQK ** -0.5)
        v = x_ref[:, 2 * ML_QK + h * ML_DV:2 * ML_QK + (h + 1) * ML_DV].astype(BF16)
        og = x_ref[:, 2 * ML_QK + ML_V + h * ML_DV:2 * ML_QK + ML_V + (h + 1) * ML_DV]

        dmat = jnp.where(causal, b_col - b_row + ig_row, -jnp.inf)
        inter = b_col + m_prev
        m_t = jnp.maximum(inter, jnp.max(dmat, -1, keepdims=True))
        w_intra = jnp.exp(dmat - m_t)
        w_inter = jnp.exp(inter - m_t)
        s = lax.dot_general(q, k.astype(BF16), (_NT, ((), ())), preferred_element_type=F32) * w_intra
        c_st = c_ref[h]
        n_st = n_ref[h][0:1, :]
        num = (jnp.dot(s.astype(BF16), v, preferred_element_type=F32)
               + w_inter * jnp.dot(q, c_st.astype(BF16), preferred_element_type=F32))
        den = jnp.sum(s, -1, keepdims=True) + w_inter * jnp.sum(q.astype(F32) * n_st, -1, keepdims=True)
        hv = num / jnp.maximum(jnp.abs(den), jnp.exp(-m_t))

        b_end = b_col[cm - 1:cm, :]
        d_end = b_end - b_col + ig_col
        m_new = jnp.maximum(b_end + m_prev, jnp.max(d_end, 0, keepdims=True))
        kw = k * jnp.exp(d_end - m_new)
        carry = jnp.exp(b_end + m_prev - m_new)
        c_ref[h] = carry * c_st + lax.dot_general(kw.astype(BF16), v, (_TN, ((), ())), preferred_element_type=F32)
        n_ref[h] = jnp.broadcast_to(carry * n_st + jnp.sum(kw, 0, keepdims=True), n_ref.shape[1:])
        m_ref[h] = jnp.broadcast_to(m_new, m_ref.shape[1:])

        hn = hv * lax.rsqrt(jnp.mean(hv * hv, -1, keepdims=True) + ML_EPS)
        sl = slice(h * ML_DV, (h + 1) * ML_DV)
        o_ref[:, sl] = (hn * ng_ref[:, sl] * jax.nn.sigmoid(og)).astype(o_ref.dtype)


def mlstm_params(w_in, b_if, norm_g, w_o):
    nmain = 2 * ML_QK + 2 * ML_V
    b_c = _pad_cols(b_if.reshape(1, 2 * ML_HEADS), LANE)
    b_r = jnp.broadcast_to(b_if.reshape(2 * ML_HEADS, 1), (2 * ML_HEADS, LANE))
    return dict(w_main=w_in[:, :nmain].astype(BF16), w_g=_pad_cols(w_in[:, nmain:], LANE).astype(BF16),
                ones=jnp.ones((1, nmain), F32), b_c=b_c, b_r=b_r, norm_g=norm_g.reshape(1, ML_V),
                w_o=w_o.astype(BF16))


def mlstm_mixer(h, p):
    lp = h.shape[0]
    cm = ML_CHUNK
    nmain = 2 * ML_QK + 2 * ML_V
    main = proj(h, p["w_main"], p["ones"], 768, F32)
    gates = proj(h, p["w_g"], p["ones"][:, :LANE], LANE, F32)
    gates_r = gates[:, :2 * ML_HEADS].T
    return pl.pallas_call(
        _mlstm_body,
        grid=(lp // cm,),
        in_specs=[pl.BlockSpec((cm, nmain), lambda i: (i, 0)),
                  pl.BlockSpec((cm, LANE), lambda i: (i, 0)),
                  pl.BlockSpec((2 * ML_HEADS, cm), lambda i: (0, i)),
                  pl.BlockSpec((1, LANE), lambda i: (0, 0)),
                  pl.BlockSpec((2 * ML_HEADS, LANE), lambda i: (0, 0)),
                  pl.BlockSpec((1, ML_V), lambda i: (0, 0))],
        out_specs=pl.BlockSpec((cm, ML_V), lambda i: (i, 0)),
        out_shape=jax.ShapeDtypeStruct((lp, ML_V), BF16),
        scratch_shapes=[pltpu.VMEM((ML_HEADS, ML_DQK, ML_DV), F32),
                        pltpu.VMEM((ML_HEADS, SUBLANE, ML_DQK), F32),
                        pltpu.VMEM((ML_HEADS, SUBLANE, LANE), F32)],
        compiler_params=_cparams(("arbitrary",)),
        name="mlstm",
    )(main, gates, gates_r, p["b_c"], p["b_r"], p["norm_g"])


def kernel(x, meta_tokens, ln_g, ln_b, ffn_in, ffn_out, rwkv_mix, rwkv_w_rkv, rwkv_w0, rwkv_w1, rwkv_w2, rwkv_a0, rwkv_a1, rwkv_a2, rwkv_g1, rwkv_g2, rwkv_k_k, rwkv_k_a, rwkv_r_k, rwkv_gn_g, rwkv_gn_b, rwkv_w_o, rg_w_in, rg_conv_w, rg_conv_b, rg_w_a, rg_b_a, rg_w_x, rg_b_x, rg_lambda, rg_w_o, fox_w_in, fox_b_f, fox_w_o, ml_w_in, ml_b_if, ml_norm_g, ml_w_o):
    batch, seq, _ = x.shape
    assert batch == 1
    l = seq + N_META
    lp = -(-l // ROW_ALIGN) * ROW_ALIGN
    h = jnp.concatenate([meta_tokens.astype(x.dtype), x[0], jnp.zeros((lp - l, D_MODEL), x.dtype)], axis=0)

    ffn_in_b = ffn_in.astype(BF16)
    ffn_out_b = ffn_out.astype(BF16)
    rw = rwkv_params(rwkv_mix, rwkv_w_rkv, rwkv_w0, rwkv_w1, rwkv_w2, rwkv_a0, rwkv_a1, rwkv_a2, rwkv_g1,
                     rwkv_g2, rwkv_k_k, rwkv_k_a, rwkv_r_k, rwkv_gn_g, rwkv_gn_b, rwkv_w_o)
    rg = rg_params(rg_w_in, rg_conv_w, rg_conv_b, rg_w_a, rg_b_a, rg_w_x, rg_b_x, rg_lambda, rg_w_o)
    fx = fox_params(fox_w_in, fox_b_f, fox_w_o)
    ml = mlstm_params(ml_w_in, ml_b_if, ml_norm_g, ml_w_o)
    mixers = ((rwkv_mixer, rw), (rg_mixer, rg), (fox_mixer, fx), (mlstm_mixer, ml))

    vec = lambda t: t.reshape(1, D_MODEL)
    for layer in range(DEPTH):
        h = ffn_ln(h, ffn_in_b, ffn_out_b, layer, 0, vec(ln_g[layer, 0]), vec(ln_b[layer, 0]))
        mixer, p = mixers[layer % len(mixers)]
        y = mixer(h, p)
        h = outproj_ln(h, y, p["w_o"], vec(ln_g[layer, 1]), vec(ln_b[layer, 1]))
        h = ffn_ln(h, ffn_in_b, ffn_out_b, layer, 1, vec(ln_g[layer, 2]), vec(ln_b[layer, 2]))
    return h[N_META:l][None]
```

```python
import functools

import numpy as np
import jax
import jax.numpy as jnp
from jax import lax
from jax.experimental import pallas as pl
from jax.experimental.pallas import tpu as pltpu

F32 = jnp.float32
BF16 = jnp.bfloat16

D_MODEL = 2048
DEPTH = 4
N_META = 16
D_FF = 5632
ALPHA = (2 * DEPTH) ** 0.25
LN_EPS = 1e-5
RWKV_HEAD_DIM = 64
RWKV_HEADS = D_MODEL // RWKV_HEAD_DIM
RWKV_GN_EPS = 64e-5
RG_BLOCKS = 16
D_RNN = 2688
RG_BLOCK_DIM = D_RNN // RG_BLOCKS
RG_C = 8.0
CONV_W = 4
FOX_HEAD_DIM = 128
FOX_HEADS = D_MODEL // FOX_HEAD_DIM
ML_HEADS = 8
ML_QK = D_MODEL // 2
ML_V = D_MODEL
ML_DQK = ML_QK // ML_HEADS
ML_DV = ML_V // ML_HEADS
ML_EPS = 1e-6

LANE = 128
SUBLANE = 8
ROW_ALIGN = 1280
VMEM_LIMIT = 56 * 1024 * 1024

FFN_TM, FFN_TN = 640, 512
PROJ_TM = 640
OUT_TM = 256
RWKV_PREP_TM, RWKV_PREP_TN = 256, 512
RWKV_CHUNK = 64
RG_TM = 256
RG_KWIN = 512
FOX_TQ = 640
FOX_STRIP = 16
ML_CHUNK = 256

HI = lax.Precision.HIGHEST
LOG2E = float(np.log2(np.e))


def _cparams(sem):
    return pltpu.CompilerParams(dimension_semantics=sem, vmem_limit_bytes=VMEM_LIMIT)


def _ln(y, g, b):
    mu = jnp.mean(y, -1, keepdims=True)
    yc = y - mu
    var = jnp.mean(yc * yc, -1, keepdims=True)
    return yc * lax.rsqrt(var + LN_EPS) * g + b


def _softplus(z):
    return jnp.maximum(z, 0.0) + jnp.log1p(jnp.exp(-jnp.abs(z)))


def _log_sigmoid(z):
    return -_softplus(-z)


def _scan_add(x, axis):
    n = x.shape[axis]
    idx = lax.broadcasted_iota(jnp.int32, x.shape, axis)
    s = 1
    while s < n:
        x = x + jnp.where(idx >= s, pltpu.roll(x, s, axis), 0.0)
        s *= 2
    return x


def _ffn_ln_body(x_ref, wg_ref, wu_ref, wo_ref, g_ref, b_ref, o_ref, xb_ref, acc_ref):
    j = pl.program_id(1)

    @pl.when(j == 0)
    def _():
        xb_ref[...] = x_ref[...].astype(BF16)
        acc_ref[...] = jnp.zeros_like(acc_ref)

    xb = xb_ref[...]
    hg = jnp.dot(xb, wg_ref[...], preferred_element_type=F32)
    hu = jnp.dot(xb, wu_ref[...], preferred_element_type=F32)
    act = (hg * jax.nn.sigmoid(hg) * hu).astype(BF16)
    acc_ref[...] += jnp.dot(act, wo_ref[...], preferred_element_type=F32)

    @pl.when(j == pl.num_programs(1) - 1)
    def _():
        y = ALPHA * x_ref[...] + 0.5 * acc_ref[...]
        o_ref[...] = _ln(y, g_ref[...], b_ref[...])


def ffn_ln(h, w_in, w_out, layer, idx, g, b):
    lp = h.shape[0]
    tm, tn = FFN_TM, FFN_TN
    nf = D_FF // tn
    return pl.pallas_call(
        _ffn_ln_body,
        grid=(lp // tm, nf),
        in_specs=[
            pl.BlockSpec((tm, D_MODEL), lambda i, j: (i, 0)),
            pl.BlockSpec((None, None, D_MODEL, tn), lambda i, j: (layer, idx, 0, j)),
            pl.BlockSpec((None, None, D_MODEL, tn), lambda i, j: (layer, idx, 0, j + nf)),
            pl.BlockSpec((None, None, tn, D_MODEL), lambda i, j: (layer, idx, j, 0)),
            pl.BlockSpec((1, D_MODEL), lambda i, j: (0, 0)),
            pl.BlockSpec((1, D_MODEL), lambda i, j: (0, 0)),
        ],
        out_specs=pl.BlockSpec((tm, D_MODEL), lambda i, j: (i, 0)),
        out_shape=jax.ShapeDtypeStruct((lp, D_MODEL), F32),
        scratch_shapes=[pltpu.VMEM((tm, D_MODEL), BF16), pltpu.VMEM((tm, D_MODEL), F32)],
        compiler_params=_cparams(("parallel", "arbitrary")),
        name="ffn_ln",
    )(h, w_in, w_in, w_out, g, b)


def _proj_body(x_ref, w_ref, sc_ref, o_ref, xb_ref):
    @pl.when(pl.program_id(1) == 0)
    def _():
        xb_ref[...] = x_ref[...].astype(BF16)

    acc = jnp.dot(xb_ref[...], w_ref[...], preferred_element_type=F32)
    o_ref[...] = (acc * sc_ref[...]).astype(o_ref.dtype)


def proj(h, w, col_scale, tn, out_dtype):
    lp, n = h.shape[0], w.shape[1]
    tm = PROJ_TM
    return pl.pallas_call(
        _proj_body,
        grid=(lp // tm, n // tn),
        in_specs=[pl.BlockSpec((tm, D_MODEL), lambda i, j: (i, 0)),
                  pl.BlockSpec((D_MODEL, tn), lambda i, j: (0, j)),
                  pl.BlockSpec((1, tn), lambda i, j: (0, j))],
        out_specs=pl.BlockSpec((tm, tn), lambda i, j: (i, j)),
        out_shape=jax.ShapeDtypeStruct((lp, n), out_dtype),
        scratch_shapes=[pltpu.VMEM((tm, D_MODEL), BF16)],
        compiler_params=_cparams(("parallel", "arbitrary")),
        name="proj",
    )(h, w, col_scale)


def _outproj_ln_body(x_ref, y_ref, w_ref, g_ref, b_ref, o_ref):
    acc = jnp.dot(y_ref[...], w_ref[...], preferred_element_type=F32)
    o_ref[...] = _ln(ALPHA * x_ref[...] + acc, g_ref[...], b_ref[...])


def outproj_ln(h, y, w, g, b):
    lp, k = y.shape
    tm = OUT_TM
    return pl.pallas_call(
        _outproj_ln_body,
        grid=(lp // tm,),
        in_specs=[pl.BlockSpec((tm, D_MODEL), lambda i: (i, 0)),
                  pl.BlockSpec((tm, k), lambda i: (i, 0)),
                  pl.BlockSpec((k, D_MODEL), lambda i: (0, 0)),
                  pl.BlockSpec((1, D_MODEL), lambda i: (0, 0)),
                  pl.BlockSpec((1, D_MODEL), lambda i: (0, 0))],
        out_specs=pl.BlockSpec((tm, D_MODEL), lambda i: (i, 0)),
        out_shape=jax.ShapeDtypeStruct((lp, D_MODEL), F32),
        compiler_params=_cparams(("parallel",)),
        name="outproj_ln",
    )(h, y, w, g, b)


def _rwkv_prep_body(x_ref, xp_ref, mix_ref, wrkv_ref, w1_ref, a1_ref, g1_ref, w2_ref, a2_ref, g2_ref,
                    w0_ref, a0_ref, kk_ref, ka_ref, bd_ref,
                    r_o, lw_o, k_o, v_o, kk_o, b_o, g_o,
                    xe_ref, xx_ref, xs_ref, tw_ref, ta_ref, tg_ref):
    i, j = pl.program_id(0), pl.program_id(1)
    tm = x_ref.shape[0]

    @pl.when(j == 0)
    def _():
        x = x_ref[...]
        xe_ref[0:SUBLANE, :] = jnp.where(i == 0, 0.0, xp_ref[...])
        xe_ref[SUBLANE:, :] = x
        xx_ref[...] = xe_ref[pl.ds(SUBLANE - 1, tm), :] - x
        for s in range(3):
            xs_ref[s] = (x + xx_ref[...] * mix_ref[s:s + 1, :]).astype(BF16)
        xs_w = (x + xx_ref[...] * mix_ref[3:4, :]).astype(BF16)
        xs_a = (x + xx_ref[...] * mix_ref[4:5, :]).astype(BF16)
        xs_g = (x + xx_ref[...] * mix_ref[5:6, :]).astype(BF16)
        tw_ref[...] = jnp.tanh(jnp.dot(xs_w, w1_ref[...], preferred_element_type=F32)).astype(BF16)
        ta_ref[...] = jnp.dot(xs_a, a1_ref[...], preferred_element_type=F32).astype(BF16)
        tg_ref[...] = jax.nn.sigmoid(jnp.dot(xs_g, g1_ref[...], preferred_element_type=F32)).astype(BF16)

    r = jnp.dot(xs_ref[0], wrkv_ref[0], preferred_element_type=F32)
    k = jnp.dot(xs_ref[1], wrkv_ref[1], preferred_element_type=F32)
    v = jnp.dot(xs_ref[2], wrkv_ref[2], preferred_element_type=F32)
    wz = w0_ref[...] + jnp.dot(tw_ref[...], w2_ref[...], preferred_element_type=F32)
    az = a0_ref[...] + jnp.dot(ta_ref[...], a2_ref[...], preferred_element_type=F32)
    g = jnp.dot(tg_ref[...], g2_ref[...], preferred_element_type=F32)

    w_log = -_softplus(-wz) - 0.5
    a = jax.nn.sigmoid(az)
    kk = k * kk_ref[...]
    kk2 = kk * kk
    bd = bd_ref[...]
    kk2_hi = kk2.astype(BF16)
    kk2_lo = (kk2 - kk2_hi.astype(F32)).astype(BF16)
    ss = jnp.concatenate(
        [jnp.dot(jnp.concatenate([kk2_hi[:, c * LANE:(c + 1) * LANE], kk2_lo[:, c * LANE:(c + 1) * LANE]], axis=1),
                 bd, preferred_element_type=F32)
         for c in range(kk2.shape[1] // LANE)], axis=1)
    kkn = kk * lax.rsqrt(ss + 1e-12)

    r_o[...] = r
    lw_o[...] = -jnp.exp(w_log)
    k_o[...] = k * (1.0 + (a - 1.0) * ka_ref[...])
    v_o[...] = v
    kk_o[...] = kkn
    b_o[...] = kkn * a
    g_o[...] = g


def rwkv_prep(h, p):
    lp = h.shape[0]
    tm, tn = RWKV_PREP_TM, RWKV_PREP_TN
    row = lambda i, j: (i, 0)
    col = lambda i, j: (0, j)
    full = lambda i, j: (0, 0)
    lw, la, lg = p["w1"].shape[1], p["a1"].shape[1], p["g1"].shape[1]
    out = jax.ShapeDtypeStruct((lp, D_MODEL), F32)
    return pl.pallas_call(
        _rwkv_prep_body,
        grid=(lp // tm, D_MODEL // tn),
        in_specs=[
            pl.BlockSpec((tm, D_MODEL), row),
            pl.BlockSpec((SUBLANE, D_MODEL), lambda i, j: (jnp.maximum(i * (tm // SUBLANE) - 1, 0), 0)),
            pl.BlockSpec((SUBLANE, D_MODEL), full),
            pl.BlockSpec((3, D_MODEL, tn), lambda i, j: (0, 0, j)),
            pl.BlockSpec((D_MODEL, lw), full),
            pl.BlockSpec((D_MODEL, la), full),
            pl.BlockSpec((D_MODEL, lg), full),
            pl.BlockSpec((lw, tn), col),
            pl.BlockSpec((la, tn), col),
            pl.BlockSpec((lg, tn), col),
            pl.BlockSpec((1, tn), col),
            pl.BlockSpec((1, tn), col),
            pl.BlockSpec((1, tn), col),
            pl.BlockSpec((1, tn), col),
            pl.BlockSpec((2 * LANE, LANE), full),
        ],
        out_specs=[pl.BlockSpec((tm, tn), lambda i, j: (i, j))] * 7,
        out_shape=[out] * 7,
        scratch_shapes=[pltpu.VMEM((tm + SUBLANE, D_MODEL), F32), pltpu.VMEM((tm, D_MODEL), F32),
                        pltpu.VMEM((3, tm, D_MODEL), BF16),
                        pltpu.VMEM((tm, lw), BF16), pltpu.VMEM((tm, la), BF16), pltpu.VMEM((tm, lg), BF16)],
        compiler_params=_cparams(("parallel", "arbitrary")),
        name="rwkv_prep",
    )(h, h, p["mix"], p["w_rkv"], p["w1"], p["a1"], p["g1"], p["w2"], p["a2"], p["g2"],
      p["w0"], p["a0"], p["k_k"], p["k_a"], p["bd"])


RWKV_GROUP = 4
RWKV_GW = RWKV_GROUP * RWKV_HEAD_DIM
RWKV_SOLVE_BASE = 8

_NN = ((1,), (0,))
_NT = ((1,), (1,))
_TN = ((0,), (0,))


def _mm(a, b, dims):
    return lax.dot_general(a.astype(BF16), b.astype(BF16), (dims, ((), ())), preferred_element_type=F32)


def _rwkv_chunk_body(r_ref, lw_ref, k_ref, v_ref, kk_ref, b_ref, g_ref, rk_ref, gng_ref, gnb_ref, tri_ref,
                     o_ref, s_ref):
    c = r_ref.shape[0]
    n = RWKV_HEAD_DIM
    gw = RWKV_GW

    @pl.when(pl.program_id(0) == 0)
    def _():
        s_ref[...] = jnp.zeros_like(s_ref)

    lw = lw_ref[...]
    cum = jnp.dot(tri_ref[...], lw, precision=HI, preferred_element_type=F32)
    e_pos = jnp.exp(cum)
    e_neg = jnp.exp(-cum)
    e_end = e_pos[c - 1:c, :]
    at_all = -kk_ref[...] * jnp.exp(cum - lw)
    rt_all = r_ref[...] * e_pos
    bb_all = b_ref[...] * e_neg
    kb_all = k_ref[...] * e_neg
    rkr_all = r_ref[...] * k_ref[...] * rk_ref[...]

    blk = (lax.broadcasted_iota(jnp.int32, (gw, gw), 0) // n == lax.broadcasted_iota(jnp.int32, (gw, gw), 1) // n)
    blk = blk.astype(F32).astype(BF16)
    row = lax.broadcasted_iota(jnp.int32, (c, gw), 0)
    src = lax.broadcasted_iota(jnp.int32, (c, gw), 1) % n
    strict = src < row
    incl = src <= row

    def bdiag(x):
        return jnp.tile(x.astype(BF16), (RWKV_GROUP, 1)) * blk

    def seg_sums(xs):
        hi = [x.astype(BF16) for x in xs]
        lo = [(x - h_.astype(F32)).astype(BF16) for x, h_ in zip(xs, hi)]
        tot = jnp.dot(jnp.concatenate(hi + lo, axis=0), blk, preferred_element_type=F32)
        rows = xs[0].shape[0]
        return [tot[i * rows:(i + 1) * rows] + tot[(len(xs) + i) * rows:(len(xs) + i + 1) * rows]
                for i in range(len(xs))]

    groups = [slice(g * gw, (g + 1) * gw) for g in range(D_MODEL // gw)]
    v_g = [v_ref[:, sl] for sl in groups]
    lhs = [jnp.concatenate([at_all[:, sl], rt_all[:, sl]], axis=0).astype(BF16) for sl in groups]
    g_b = [_mm(a, bdiag(bb_all[:, sl]), _NT) for a, sl in zip(lhs, groups)]
    g_k = [_mm(a, bdiag(kb_all[:, sl]), _NT) for a, sl in zip(lhs, groups)]
    s0 = [s_ref[g] for g in range(len(groups))]
    z = [_mm(a, bdiag(s_), _NT) for a, s_ in zip(lhs, s0)]
    a_ab = [jnp.where(strict, gm[:c], 0.0) for gm in g_b]
    a_ak = [jnp.where(strict, gm[:c], 0.0) for gm in g_k]
    a_r = [jnp.concatenate([jnp.where(incl, gb[c:], 0.0), jnp.where(incl, gk[c:], 0.0)], axis=1).astype(BF16)
           for gb, gk in zip(g_b, g_k)]
    bd_v = [bdiag(v_) for v_ in v_g]
    rhs = [z_[:c] + _mm(a, bv, _NN) for z_, a, bv in zip(z, a_ak, bd_v)]
    def within(size):
        return (row // size) == (src // size)

    tinv = [jnp.where(within(RWKV_SOLVE_BASE), a, 0.0) for a in a_ab]
    pw = tinv
    tinv = [(src == row).astype(F32) + t_ for t_ in tinv]
    span = 2
    while span < RWKV_SOLVE_BASE:
        pw = [_mm(p_, bdiag(p_), _NN) for p_ in pw]
        tinv = [t_ + _mm(p_, bdiag(t_), _NN) for t_, p_ in zip(tinv, pw)]
        span *= 2
    size = RWKV_SOLVE_BASE
    while size < c:
        link = within(2 * size) & jnp.logical_not(within(size))
        bd_t = [bdiag(t_) for t_ in tinv]
        tl = [_mm(t_, bdiag(jnp.where(link, a, 0.0)), _NN) for t_, a in zip(tinv, a_ab)]
        tinv = [t_ + _mm(x_, b_, _NN) for t_, x_, b_ in zip(tinv, tl, bd_t)]
        size *= 2
    u = [_mm(t_, bdiag(r_), _NN) for t_, r_ in zip(tinv, rhs)]
    y = [z_[c:] + _mm(a, jnp.concatenate([bdiag(u_), bv], axis=0), _NN)
         for z_, a, u_, bv in zip(z, a_r, u, bd_v)]
    for g, sl in enumerate(groups):
        e_end_g = e_end[:, sl]
        uv = jnp.concatenate([u[g], v_g[g]], axis=0)
        bk = jnp.concatenate([bb_all[:, sl], kb_all[:, sl]], axis=0) * e_end_g
        full = _mm(uv, bk, _TN) * blk.astype(F32)
        s_ref[g] = s0[g] * e_end_g + functools.reduce(
            lambda p_, q_: p_ + q_, [full[q * n:(q + 1) * n, :] for q in range(RWKV_GROUP)])
    sums = seg_sums(y + [rkr_all[:, sl] for sl in groups])
    yc = [y_ - m_ * (1.0 / n) for y_, m_ in zip(y, sums[:len(groups)])]
    var = seg_sums([t * t for t in yc])
    for g, sl in enumerate(groups):
        yn = yc[g] * lax.rsqrt(var[g] * (1.0 / n) + RWKV_GN_EPS) * gng_ref[:, sl] + gnb_ref[:, sl]
        bonus = sums[len(groups) + g] * v_g[g]
        o_ref[:, sl] = ((yn + bonus) * g_ref[:, sl]).astype(o_ref.dtype)


def rwkv_chunk(r, lw, k, v, kk, b, g, r_k, gn_g, gn_b):
    lp = r.shape[0]
    c = RWKV_CHUNK
    tri = jnp.asarray(np.tril(np.ones((c, c), np.float32)))
    blk = pl.BlockSpec((c, D_MODEL), lambda i: (i, 0))
    vec = pl.BlockSpec((1, D_MODEL), lambda i: (0, 0))
    return pl.pallas_call(
        _rwkv_chunk_body,
        grid=(lp // c,),
        in_specs=[blk] * 7 + [vec] * 3 + [pl.BlockSpec((c, c), lambda i: (0, 0))],
        out_specs=blk,
        out_shape=jax.ShapeDtypeStruct((lp, D_MODEL), BF16),
        scratch_shapes=[pltpu.VMEM((D_MODEL // RWKV_GW, RWKV_HEAD_DIM, RWKV_GW), F32)],
        compiler_params=_cparams(("arbitrary",)),
        name="rwkv_chunk",
    )(r, lw, k, v, kk, b, g, r_k, gn_g, gn_b, tri)


def _pad_rows(a, rows):
    return jnp.pad(a, ((0, rows - a.shape[0]), (0, 0)))


def _pad_cols(a, cols):
    return jnp.pad(a, ((0, 0), (0, cols - a.shape[1])))


def rwkv_params(mix, w_rkv, w0, w1, w2, a0, a1, a2, g1, g2, k_k, k_a, r_k, gn_g, gn_b, w_o):
    lora = LANE
    bd = np.kron(np.eye(LANE // RWKV_HEAD_DIM, dtype=np.float32),
                 np.ones((RWKV_HEAD_DIM, RWKV_HEAD_DIM), np.float32))
    bd = np.concatenate([bd, bd], axis=0)
    row = lambda t: t.reshape(1, D_MODEL)
    return dict(
        mix=_pad_rows(mix, SUBLANE), w_rkv=w_rkv.astype(BF16),
        w1=_pad_cols(w1, lora).astype(BF16), w2=_pad_rows(w2, lora).astype(BF16),
        a1=_pad_cols(a1, lora).astype(BF16), a2=_pad_rows(a2, lora).astype(BF16),
        g1=g1.astype(BF16), g2=g2.astype(BF16),
        w0=row(w0), a0=row(a0), k_k=row(k_k), k_a=row(k_a), bd=jnp.asarray(bd, BF16),
        r_k=row(r_k), gn_g=row(gn_g), gn_b=row(gn_b), w_o=w_o.astype(BF16))


def rwkv_mixer(h, p):
    r, lw, k, v, kk, b, g = rwkv_prep(h, p)
    return rwkv_chunk(r, lw, k, v, kk, b, g, p["r_k"], p["gn_g"], p["gn_b"])


def _rg_windows():
    starts = []
    for c in range(D_RNN // LANE):
        first_block = (c * LANE) // RG_BLOCK_DIM
        ks = (first_block * RG_BLOCK_DIM) // LANE * LANE
        starts.append(min(ks, D_RNN - RG_KWIN))
    return starts


def _rg_compact(w):
    eye = jnp.eye(RG_BLOCKS, dtype=w.dtype)
    dense = (eye[:, None, :, None] * w[:, :, None, :]).reshape(D_RNN, D_RNN)
    tiles = [dense[ks:ks + RG_KWIN, c * LANE:(c + 1) * LANE] for c, ks in enumerate(_rg_windows())]
    return jnp.stack(tiles).astype(BF16)


def _gelu_tanh(x):
    return 0.5 * x * (1.0 + jnp.tanh(np.sqrt(2.0 / np.pi).astype(np.float32) * (x + 0.044715 * (x * x * x))))


def _rg_body(gate_ref, u_ref, up_ref, cw_ref, cb_ref, wa_ref, wx_ref, ba_ref, bx_ref, lam_ref, o_ref,
             ue_ref, carry_ref):
    i = pl.program_id(0)
    tm = u_ref.shape[0]

    @pl.when(i == 0)
    def _():
        carry_ref[...] = jnp.zeros_like(carry_ref)

    u = u_ref[...]
    ue_ref[0:SUBLANE, :] = jnp.where(i == 0, 0.0, up_ref[...])
    ue_ref[SUBLANE:, :] = u
    uc = cb_ref[...] + u * cw_ref[CONV_W - 1:CONV_W, :]
    for d in range(1, CONV_W):
        uc = uc + ue_ref[pl.ds(SUBLANE - d, tm), :] * cw_ref[CONV_W - 1 - d:CONV_W - d, :]
    ucb = uc.astype(BF16)
    za, zx = [], []
    for c, ks in enumerate(_rg_windows()):
        win = ucb[:, ks:ks + RG_KWIN]
        za.append(jnp.dot(win, wa_ref[c], preferred_element_type=F32))
        zx.append(jnp.dot(win, wx_ref[c], preferred_element_type=F32))
    r = jax.nn.sigmoid(jnp.concatenate(za, axis=1) + ba_ref[...])
    ig = jax.nn.sigmoid(jnp.concatenate(zx, axis=1) + bx_ref[...])
    log_a = -RG_C * r * _softplus(-lam_ref[...])
    a = jnp.exp(log_a)
    w2 = -2.0 * jnp.tanh(log_a)
    q = w2 * (1.0 + 0.5 * w2)
    b = jnp.where(q > 0.0, w2 * lax.rsqrt(q), 0.0) * (ig * uc)
    groups = tm // SUBLANE
    a = a.reshape(groups, SUBLANE, D_RNN)
    b = b.reshape(groups, SUBLANE, D_RNN)
    sub = lax.broadcasted_iota(jnp.int32, a.shape, 1)
    s = 1
    while s < SUBLANE:
        keep = sub >= s
        b = jnp.where(keep, a * pltpu.roll(b, s, 1) + b, b)
        a = jnp.where(keep, a * pltpu.roll(a, s, 1), a)
        s *= 2
    carry = carry_ref[0:1, :]
    for g in range(groups):
        hg = a[g] * carry + b[g]
        rows = pl.ds(g * SUBLANE, SUBLANE)
        o_ref[rows, :] = (_gelu_tanh(gate_ref[rows, :]) * hg).astype(o_ref.dtype)
        carry = hg[SUBLANE - 1:SUBLANE, :]
    carry_ref[...] = jnp.broadcast_to(carry, carry_ref.shape)


def rg_params(w_in, conv_w, conv_b, w_a, b_a, w_x, b_x, lam, w_o):
    row = lambda t: t.reshape(1, D_RNN)
    return dict(w_in=w_in.astype(BF16), conv_w=_pad_rows(conv_w, SUBLANE), conv_b=row(conv_b),
                w_a=_rg_compact(w_a), w_x=_rg_compact(w_x), b_a=row(b_a), b_x=row(b_x),
                lam=row(lam), w_o=w_o.astype(BF16), ones=jnp.ones((1, 2 * D_RNN), F32))


def rg_mixer(h, p):
    lp = h.shape[0]
    tm = RG_TM
    gu = proj(h, p["w_in"], p["ones"], 768, F32)
    ntile = D_RNN // LANE
    vec = pl.BlockSpec((1, D_RNN), lambda i: (0, 0))
    wspec = pl.BlockSpec((ntile, RG_KWIN, LANE), lambda i: (0, 0, 0))
    return pl.pallas_call(
        _rg_body,
        grid=(lp // tm,),
        in_specs=[pl.BlockSpec((tm, D_RNN), lambda i: (i, 0)),
                  pl.BlockSpec((tm, D_RNN), lambda i: (i, 1)),
                  pl.BlockSpec((SUBLANE, D_RNN), lambda i: (jnp.maximum(i * (tm // SUBLANE) - 1, 0), 1)),
                  pl.BlockSpec((SUBLANE, D_RNN), lambda i: (0, 0)),
                  vec, wspec, wspec, vec, vec, vec],
        out_specs=pl.BlockSpec((tm, D_RNN), lambda i: (i, 0)),
        out_shape=jax.ShapeDtypeStruct((lp, D_RNN), BF16),
        scratch_shapes=[pltpu.VMEM((tm + SUBLANE, D_RNN), F32), pltpu.VMEM((SUBLANE, D_RNN), F32)],
        compiler_params=_cparams(("arbitrary",)),
        name="rg_lru",
    )(gu, gu, gu, p["conv_w"], p["conv_b"], p["w_a"], p["w_x"], p["b_a"], p["b_x"], p["lam"])


def _split3(x):
    hi = x.astype(BF16).astype(F32)
    mid = (x - hi).astype(BF16).astype(F32)
    lo = (x - hi - mid).astype(BF16).astype(F32)
    return hi, mid, lo


def _fox_gate_body(x_ref, w_ref, bf_ref, cc_ref, qx_ref, kx_ref, carry_ref):
    @pl.when(pl.program_id(0) == 0)
    def _():
        carry_ref[...] = jnp.zeros_like(carry_ref)

    tm = x_ref.shape[0]
    fl = jnp.dot(x_ref[...].astype(BF16), w_ref[...], preferred_element_type=F32)
    c = _scan_add(_log_sigmoid(fl + bf_ref[...]), 0) + carry_ref[0:1, :]
    carry_ref[...] = jnp.broadcast_to(c[tm - 1:, :], carry_ref.shape)
    c = c * LOG2E
    cc_ref[...] = jnp.broadcast_to(c[0:1, :], cc_ref.shape)
    fine = c - c[0:1, :]
    lane = lax.broadcasted_iota(jnp.int32, (tm, LANE), 1)
    ones_mid = jnp.where(lane < 6, 1.0, 0.0)
    for h in range(FOX_HEADS):
        hi, mid, lo = [jnp.broadcast_to(t, (tm, LANE)) for t in _split3(fine[:, h:h + 1])]
        qx = jnp.where(lane == 0, hi, jnp.where(lane == 1, mid, lo))
        kx = -jnp.where(lane == 3, hi, jnp.where(lane == 4, mid, lo))
        qx_ref[:, h * LANE:(h + 1) * LANE] = jnp.where(lane < 3, qx, ones_mid).astype(BF16)
        kx_ref[:, h * LANE:(h + 1) * LANE] = jnp.where(lane < 3, 1.0, jnp.where(lane < 6, kx, 0.0)).astype(BF16)


def fox_gate(h, w_f, b_f):
    lp = h.shape[0]
    tm = FOX_TQ
    nt = lp // tm
    wide = jax.ShapeDtypeStruct((lp, FOX_HEADS * LANE), BF16)
    return pl.pallas_call(
        _fox_gate_body,
        grid=(nt,),
        in_specs=[pl.BlockSpec((tm, D_MODEL), lambda i: (i, 0)),
                  pl.BlockSpec((D_MODEL, LANE), lambda i: (0, 0)),
                  pl.BlockSpec((1, LANE), lambda i: (0, 0))],
        out_specs=[pl.BlockSpec((SUBLANE, LANE), lambda i: (i, 0)),
                   pl.BlockSpec((tm, FOX_HEADS * LANE), lambda i: (i, 0)),
                   pl.BlockSpec((tm, FOX_HEADS * LANE), lambda i: (i, 0))],
        out_shape=[jax.ShapeDtypeStruct((nt * SUBLANE, LANE), F32), wide, wide],
        scratch_shapes=[pltpu.VMEM((SUBLANE, LANE), F32)],
        compiler_params=_cparams(("arbitrary",)),
        name="fox_gate",
    )(h, w_f, b_f)


def _fox_attn_body(cc_ref, q_ref, qx_ref, k_ref, kx_ref, v_ref, o_ref,
                   s_ref, p_ref, mx_ref, d_ref, m_ref, corr_ref, acc_ref):
    h, i = pl.program_id(0), pl.program_id(1)
    tq = q_ref.shape[0]
    nl = tq // LANE
    q_aug = jnp.concatenate([q_ref[...], qx_ref[...]], axis=1)
    ones_col = (lax.broadcasted_iota(jnp.int32, (tq, LANE), 1) == 0).astype(BF16)
    strips = list(range(0, tq, FOX_STRIP))

    def tile_rows(j):
        return pl.ds(pl.multiple_of(j * tq, tq), tq)

    def logits(j, slot):
        rows = tile_rows(j)
        k_aug = jnp.concatenate([k_ref[rows, :], kx_ref[rows, :]], axis=1)
        s_ref[slot] = lax.dot_general(q_aug, k_aug, (_NT, ((), ())), preferred_element_type=F32)

    def softmax(j, slot, diagonal):
        delta = cc_ref[h, i] - cc_ref[h, j]
        for r0 in strips:
            rows = pl.ds(r0, FOX_STRIP)
            t = s_ref[slot, rows, :]
            if diagonal:
                row = r0 + lax.broadcasted_iota(jnp.int32, t.shape, 0)
                col = lax.broadcasted_iota(jnp.int32, t.shape, 1)
                t = jnp.where(col <= row, t, -jnp.inf)
                s_ref[slot, rows, :] = t
            mx_ref[rows, :] = functools.reduce(jnp.maximum, [t[:, c * LANE:(c + 1) * LANE] for c in range(nl)])
        m_old = m_ref[...]
        m_new = jnp.maximum(m_old, jnp.max(mx_ref[...], -1, keepdims=True) + delta)
        m_ref[...] = m_new
        corr_ref[slot] = jnp.exp2(m_old - m_new)
        d_ref[...] = jnp.broadcast_to(delta - m_new, d_ref.shape)
        for r0 in strips:
            rows = pl.ds(r0, FOX_STRIP)
            e = s_ref[slot, rows, :] + jnp.tile(d_ref[rows, :], (1, nl))
            p_ref[slot, rows, :] = jnp.exp2(e).astype(BF16)

    def accumulate(j, slot):
        vb = jnp.concatenate([v_ref[tile_rows(j), :], ones_col], axis=1)
        acc_ref[...] = corr_ref[slot] * acc_ref[...] + jnp.dot(p_ref[slot], vb, preferred_element_type=F32)

    m_ref[...] = jnp.full(m_ref.shape, -jnp.inf, F32)
    acc_ref[...] = jnp.zeros_like(acc_ref)
    p_ref[1] = jnp.zeros(p_ref.shape[1:], BF16)
    corr_ref[1] = jnp.ones(corr_ref.shape[1:], F32)
    logits(0, 0)

    def pair(a, carry):
        j = 2 * a
        accumulate(jnp.maximum(j - 1, 0), 1)
        softmax(j, 0, False)
        logits(j + 1, 1)
        accumulate(j, 0)
        softmax(j + 1, 1, False)
        logits(j + 2, 0)
        return carry

    lax.fori_loop(0, i // 2, pair, 0)

    @pl.when(i % 2 == 0)
    def _():
        accumulate(jnp.maximum(i - 1, 0), 1)
        softmax(i, 0, True)
        accumulate(i, 0)

    @pl.when(i % 2 == 1)
    def _():
        accumulate(jnp.maximum(i - 2, 0), 1)
        softmax(i - 1, 0, False)
        logits(i, 1)
        accumulate(i - 1, 0)
        softmax(i, 1, True)
        accumulate(i, 1)

    acc = acc_ref[...]
    o_ref[...] = (acc[:, :FOX_HEAD_DIM] / acc[:, FOX_HEAD_DIM:FOX_HEAD_DIM + 1]).astype(o_ref.dtype)


def fox_attn(qkv, qx, kx, cc):
    lp = qkv.shape[0]
    tq = FOX_TQ
    hh = FOX_HEADS
    hd = FOX_HEAD_DIM
    return pl.pallas_call(
        _fox_attn_body,
        grid=(hh, lp // tq),
        in_specs=[pl.BlockSpec(memory_space=pltpu.SMEM),
                  pl.BlockSpec((tq, hd), lambda h, i: (i, h)),
                  pl.BlockSpec((tq, LANE), lambda h, i: (i, h)),
                  pl.BlockSpec((lp, hd), lambda h, i: (0, hh + h)),
                  pl.BlockSpec((lp, LANE), lambda h, i: (0, h)),
                  pl.BlockSpec((lp, hd), lambda h, i: (0, 2 * hh + h))],
        out_specs=pl.BlockSpec((tq, hd), lambda h, i: (i, h)),
        out_shape=jax.ShapeDtypeStruct((lp, D_MODEL), BF16),
        scratch_shapes=[pltpu.VMEM((2, tq, tq), F32), pltpu.VMEM((2, tq, tq), BF16), pltpu.VMEM((tq, LANE), F32),
                        pltpu.VMEM((tq, LANE), F32), pltpu.VMEM((tq, 1), F32), pltpu.VMEM((2, tq, 1), F32),
                        pltpu.VMEM((tq, hd + LANE), F32)],
        compiler_params=_cparams(("parallel", "arbitrary")),
        name="fox_attn",
    )(cc, qkv, qx, qkv, kx, qkv)


def fox_params(w_in, b_f, w_o):
    q_scale = jnp.full((1, D_MODEL), LOG2E * FOX_HEAD_DIM ** -0.5, F32)
    return dict(w_qkv=w_in[:, :3 * D_MODEL].astype(BF16),
                qkv_scale=jnp.concatenate([q_scale, jnp.ones((1, 2 * D_MODEL), F32)], axis=1),
                w_f=_pad_cols(w_in[:, 3 * D_MODEL:], LANE).astype(BF16),
                b_f=_pad_cols(b_f.reshape(1, FOX_HEADS), LANE), w_o=w_o.astype(BF16))


def fox_mixer(h, p):
    qkv = proj(h, p["w_qkv"], p["qkv_scale"], 768, BF16)
    cc, qx, kx = fox_gate(h, p["w_f"], p["b_f"])
    cc = cc[::SUBLANE, :FOX_HEADS].T
    return fox_attn(qkv, qx, kx, cc)


def _mlstm_body(x_ref, gc_ref, gr_ref, bc_ref, br_ref, ng_ref, o_ref, c_ref, n_ref, m_ref):
    cm = x_ref.shape[0]
    hh = ML_HEADS

    @pl.when(pl.program_id(0) == 0)
    def _():
        c_ref[...] = jnp.zeros_like(c_ref)
        n_ref[...] = jnp.zeros_like(n_ref)
        m_ref[...] = jnp.zeros_like(m_ref)

    gc = gc_ref[...] + bc_ref[...]
    bcum_c = _scan_add(_log_sigmoid(gc), 0)
    gr = gr_ref[...] + br_ref[:, 0:1]
    bcum_r = _scan_add(_log_sigmoid(gr), 1)
    row = lax.broadcasted_iota(jnp.int32, (cm, cm), 0)
    col = lax.broadcasted_iota(jnp.int32, (cm, cm), 1)
    causal = col <= row

    for h in range(hh):
        b_col = bcum_c[:, hh + h:hh + h + 1]
        ig_col = gc[:, h:h + 1]
        b_row = bcum_r[hh + h:hh + h + 1, :]
        ig_row = gr[h:h + 1, :]
        m_prev = m_ref[h][0:1, 0:1]
        q = x_ref[:, h * ML_DQK:(h + 1) * ML_DQK].astype(BF16)
        k = x_ref[:, ML_QK + h * ML_DQK:ML_QK + (h + 1) * ML_DQK] * (ML_DQK ** -0.5)
        v = x_ref[:, 2 * ML_QK + h * ML_DV:2 * ML_QK + (h + 1) * ML_DV].astype(BF16)
        og = x_ref[:, 2 * ML_QK + ML_V + h * ML_DV:2 * ML_QK + ML_V + (h + 1) * ML_DV]

        dmat = jnp.where(causal, b_col - b_row + ig_row, -jnp.inf)
        inter = b_col + m_prev
        m_t = jnp.maximum(inter, jnp.max(dmat, -1, keepdims=True))
        w_intra = jnp.exp(dmat - m_t)
        w_inter = jnp.exp(inter - m_t)
        s = lax.dot_general(q, k.astype(BF16), (_NT, ((), ())), preferred_element_type=F32) * w_intra
        c_st = c_ref[h]
        n_st = n_ref[h][0:1, :]
        num = (jnp.dot(s.astype(BF16), v, preferred_element_type=F32)
               + w_inter * jnp.dot(q, c_st.astype(BF16), preferred_element_type=F32))
        den = jnp.sum(s, -1, keepdims=True) + w_inter * jnp.sum(q.astype(F32) * n_st, -1, keepdims=True)
        hv = num / jnp.maximum(jnp.abs(den), jnp.exp(-m_t))

        b_end = b_col[cm - 1:cm, :]
        d_end = b_end - b_col + ig_col
        m_new = jnp.maximum(b_end + m_prev, jnp.max(d_end, 0, keepdims=True))
        kw = k * jnp.exp(d_end - m_new)
        carry = jnp.exp(b_end + m_prev - m_new)
        c_ref[h] = carry * c_st + lax.dot_general(kw.astype(BF16), v, (_TN, ((), ())), preferred_element_type=F32)
        n_ref[h] = jnp.broadcast_to(carry * n_st + jnp.sum(kw, 0, keepdims=True), n_ref.shape[1:])
        m_ref[h] = jnp.broadcast_to(m_new, m_ref.shape[1:])

        hn = hv * lax.rsqrt(jnp.mean(hv * hv, -1, keepdims=True) + ML_EPS)
        sl = slice(h * ML_DV, (h + 1) * ML_DV)
        o_ref[:, sl] = (hn * ng_ref[:, sl] * jax.nn.sigmoid(og)).astype(o_ref.dtype)


def mlstm_params(w_in, b_if, norm_g, w_o):
    nmain = 2 * ML_QK + 2 * ML_V
    b_c = _pad_cols(b_if.reshape(1, 2 * ML_HEADS), LANE)
    b_r = jnp.broadcast_to(b_if.reshape(2 * ML_HEADS, 1), (2 * ML_HEADS, LANE))
    return dict(w_main=w_in[:, :nmain].astype(BF16), w_g=_pad_cols(w_in[:, nmain:], LANE).astype(BF16),
                ones=jnp.ones((1, nmain), F32), b_c=b_c, b_r=b_r, norm_g=norm_g.reshape(1, ML_V),
                w_o=w_o.astype(BF16))


def mlstm_mixer(h, p):
    lp = h.shape[0]
    cm = ML_CHUNK
    nmain = 2 * ML_QK + 2 * ML_V
    main = proj(h, p["w_main"], p["ones"], 768, F32)
    gates = proj(h, p["w_g"], p["ones"][:, :LANE], LANE, F32)
    gates_r = gates[:, :2 * ML_HEADS].T
    return pl.pallas_call(
        _mlstm_body,
        grid=(lp // cm,),
        in_specs=[pl.BlockSpec((cm, nmain), lambda i: (i, 0)),
                  pl.BlockSpec((cm, LANE), lambda i: (i, 0)),
                  pl.BlockSpec((2 * ML_HEADS, cm), lambda i: (0, i)),
                  pl.BlockSpec((1, LANE), lambda i: (0, 0)),
                  pl.BlockSpec((2 * ML_HEADS, LANE), lambda i: (0, 0)),
                  pl.BlockSpec((1, ML_V), lambda i: (0, 0))],
        out_specs=pl.BlockSpec((cm, ML_V), lambda i: (i, 0)),
        out_shape=jax.ShapeDtypeStruct((lp, ML_V), BF16),
        scratch_shapes=[pltpu.VMEM((ML_HEADS, ML_DQK, ML_DV), F32),
                        pltpu.VMEM((ML_HEADS, SUBLANE, ML_DQK), F32),
                        pltpu.VMEM((ML_HEADS, SUBLANE, LANE), F32)],
        compiler_params=_cparams(("arbitrary",)),
        name="mlstm",
    )(main, gates, gates_r, p["b_c"], p["b_r"], p["norm_g"])


def kernel(x, meta_tokens, ln_g, ln_b, ffn_in, ffn_out, rwkv_mix, rwkv_w_rkv, rwkv_w0, rwkv_w1, rwkv_w2, rwkv_a0, rwkv_a1, rwkv_a2, rwkv_g1, rwkv_g2, rwkv_k_k, rwkv_k_a, rwkv_r_k, rwkv_gn_g, rwkv_gn_b, rwkv_w_o, rg_w_in, rg_conv_w, rg_conv_b, rg_w_a, rg_b_a, rg_w_x, rg_b_x, rg_lambda, rg_w_o, fox_w_in, fox_b_f, fox_w_o, ml_w_in, ml_b_if, ml_norm_g, ml_w_o):
    batch, seq, _ = x.shape
    assert batch == 1
    l = seq + N_META
    lp = -(-l // ROW_ALIGN) * ROW_ALIGN
    h = jnp.concatenate([meta_tokens.astype(x.dtype), x[0], jnp.zeros((lp - l, D_MODEL), x.dtype)], axis=0)

    ffn_in_b = ffn_in.astype(BF16)
    ffn_out_b = ffn_out.astype(BF16)
    rw = rwkv_params(rwkv_mix, rwkv_w_rkv, rwkv_w0, rwkv_w1, rwkv_w2, rwkv_a0, rwkv_a1, rwkv_a2, rwkv_g1,
                     rwkv_g2, rwkv_k_k, rwkv_k_a, rwkv_r_k, rwkv_gn_g, rwkv_gn_b, rwkv_w_o)
    rg = rg_params(rg_w_in, rg_conv_w, rg_conv_b, rg_w_a, rg_b_a, rg_w_x, rg_b_x, rg_lambda, rg_w_o)
    fx = fox_params(fox_w_in, fox_b_f, fox_w_o)
    ml = mlstm_params(ml_w_in, ml_b_if, ml_norm_g, ml_w_o)
    mixers = ((rwkv_mixer, rw), (rg_mixer, rg), (fox_mixer, fx), (mlstm_mixer, ml))

    vec = lambda t: t.reshape(1, D_MODEL)
    for layer in range(DEPTH):
        h = ffn_ln(h, ffn_in_b, ffn_out_b, layer, 0, vec(ln_g[layer, 0]), vec(ln_b[layer, 0]))
        mixer, p = mixers[layer % len(mixers)]
        y = mixer(h, p)
        h = outproj_ln(h, y, p["w_o"], vec(ln_g[layer, 1]), vec(ln_b[layer, 1]))
        h = ffn_ln(h, ffn_in_b, ffn_out_b, layer, 1, vec(ln_g[layer, 2]), vec(ln_b[layer, 2]))
    return h[N_META:l][None]
```

```python
import functools

import numpy as np
import jax
import jax.numpy as jnp
from jax import lax
from jax.experimental import pallas as pl
from jax.experimental.pallas import tpu as pltpu

F32 = jnp.float32
BF16 = jnp.bfloat16

D_MODEL = 2048
DEPTH = 4
N_META = 16
D_FF = 5632
ALPHA = (2 * DEPTH) ** 0.25
LN_EPS = 1e-5
RWKV_HEAD_DIM = 64
RWKV_HEADS = D_MODEL // RWKV_HEAD_DIM
RWKV_GN_EPS = 64e-5
RG_BLOCKS = 16
D_RNN = 2688
RG_BLOCK_DIM = D_RNN // RG_BLOCKS
RG_C = 8.0
CONV_W = 4
FOX_HEAD_DIM = 128
FOX_HEADS = D_MODEL // FOX_HEAD_DIM
ML_HEADS = 8
ML_QK = D_MODEL // 2
ML_V = D_MODEL
ML_DQK = ML_QK // ML_HEADS
ML_DV = ML_V // ML_HEADS
ML_EPS = 1e-6

LANE = 128
SUBLANE = 8
ROW_ALIGN = 1280
VMEM_LIMIT = 58 * 1024 * 1024

FFN_TM, FFN_TN = 640, 512
PROJ_TM = 640
OUT_TM = 256
RWKV_PREP_TM, RWKV_PREP_TN = 640, 256
RWKV_CHUNK = 64
RG_TM = 256
RG_KWIN = 512
FOX_TQ = 640
FOX_STRIP = 16
ML_CHUNK = 256

HI = lax.Precision.HIGHEST
LOG2E = float(np.log2(np.e))


def _cparams(sem):
    return pltpu.CompilerParams(dimension_semantics=sem, vmem_limit_bytes=VMEM_LIMIT)


def _ln(y, g, b):
    mu = jnp.mean(y, -1, keepdims=True)
    yc = y - mu
    var = jnp.mean(yc * yc, -1, keepdims=True)
    return yc * lax.rsqrt(var + LN_EPS) * g + b


def _softplus(z):
    return jnp.maximum(z, 0.0) + jnp.log1p(jnp.exp(-jnp.abs(z)))


def _log_sigmoid(z):
    return -_softplus(-z)


def _scan_add(x, axis):
    n = x.shape[axis]
    idx = lax.broadcasted_iota(jnp.int32, x.shape, axis)
    s = 1
    while s < n:
        x = x + jnp.where(idx >= s, pltpu.roll(x, s, axis), 0.0)
        s *= 2
    return x


def _ffn_ln_body(xb_ref, xp_ref, wg_ref, wu_ref, wo_ref, g_ref, b_ref, o_ref, ob_ref, acc_a, acc_b):
    s = pl.program_id(0)
    nf = D_FF // FFN_TN
    i, j = s // nf, s % nf
    rows_per_step = FFN_TM // (nf - 1)
    n_tiles = (pl.num_programs(0) - (nf - 1)) // nf

    def slice_rows(k):
        return pl.ds(pl.multiple_of(k * rows_per_step, rows_per_step), rows_per_step)

    def ln_slice(acc_prev):
        rows = slice_rows(jnp.minimum(j, nf - 2))
        y = _ln(ALPHA * xp_ref[...] + 0.5 * acc_prev[rows, :], g_ref[...], b_ref[...])
        o_ref[...] = y
        ob_ref[...] = y.astype(BF16)

    def step(acc_cur, acc_prev):
        @pl.when(s == 0)
        def _():
            acc_cur[...] = jnp.zeros_like(acc_cur)
            acc_prev[...] = jnp.zeros_like(acc_prev)

        @pl.when(i < n_tiles)
        def _():
            ln_slice(acc_prev)
            acc_prev[slice_rows(jnp.maximum(j - 1, 0)), :] = jnp.zeros((rows_per_step, D_MODEL), F32)
            xb = xb_ref[...]
            hg = jnp.dot(xb, wg_ref[...], preferred_element_type=F32)
            hu = jnp.dot(xb, wu_ref[...], preferred_element_type=F32)
            act = (hg * jax.nn.sigmoid(hg) * hu).astype(BF16)
            acc_cur[...] += jnp.dot(act, wo_ref[...], preferred_element_type=F32)

        @pl.when(i >= n_tiles)
        def _():
            ln_slice(acc_prev)

    @pl.when(i % 2 == 0)
    def _():
        step(acc_a, acc_b)

    @pl.when(i % 2 == 1)
    def _():
        step(acc_b, acc_a)


def ffn_ln(h, hb, w_in, w_out, layer, idx, g, b):
    lp = h.shape[0]
    tm, tn = FFN_TM, FFN_TN
    nf = D_FF // tn
    nm = lp // tm
    assert tm % (nf - 1) == 0
    rows = tm // (nf - 1)
    last = nm * nf - 1
    cur = lambda s: (jnp.minimum(s // nf, nm - 1), 0)
    prev = lambda s: (jnp.where(s < nf, 0, (s // nf - 1) * (nf - 1) + jnp.minimum(s % nf, nf - 2)), 0)
    wcol = lambda s: jnp.minimum(s, last) % nf
    return pl.pallas_call(
        _ffn_ln_body,
        grid=(nm * nf + nf - 1,),
        in_specs=[
            pl.BlockSpec((tm, D_MODEL), cur),
            pl.BlockSpec((rows, D_MODEL), prev),
            pl.BlockSpec((None, None, D_MODEL, tn), lambda s: (layer, idx, 0, wcol(s))),
            pl.BlockSpec((None, None, D_MODEL, tn), lambda s: (layer, idx, 0, wcol(s) + nf)),
            pl.BlockSpec((None, None, tn, D_MODEL), lambda s: (layer, idx, wcol(s), 0)),
            pl.BlockSpec((1, D_MODEL), lambda s: (0, 0)),
            pl.BlockSpec((1, D_MODEL), lambda s: (0, 0)),
        ],
        out_specs=[pl.BlockSpec((rows, D_MODEL), prev), pl.BlockSpec((rows, D_MODEL), prev)],
        out_shape=[jax.ShapeDtypeStruct((lp, D_MODEL), F32), jax.ShapeDtypeStruct((lp, D_MODEL), BF16)],
        scratch_shapes=[pltpu.VMEM((tm, D_MODEL), F32), pltpu.VMEM((tm, D_MODEL), F32)],
        compiler_params=_cparams(("arbitrary",)),
        name="ffn_ln",
    )(hb, h, w_in, w_in, w_out, g, b)


def _proj_body(x_ref, w_ref, sc_ref, o_ref, xb_ref):
    @pl.when(pl.program_id(1) == 0)
    def _():
        xb_ref[...] = x_ref[...].astype(BF16)

    acc = jnp.dot(xb_ref[...], w_ref[...], preferred_element_type=F32)
    o_ref[...] = (acc * sc_ref[...]).astype(o_ref.dtype)


def proj(h, w, col_scale, tn, out_dtype):
    lp, n = h.shape[0], w.shape[1]
    tm = PROJ_TM
    return pl.pallas_call(
        _proj_body,
        grid=(lp // tm, n // tn),
        in_specs=[pl.BlockSpec((tm, D_MODEL), lambda i, j: (i, 0)),
                  pl.BlockSpec((D_MODEL, tn), lambda i, j: (0, j)),
                  pl.BlockSpec((1, tn), lambda i, j: (0, j))],
        out_specs=pl.BlockSpec((tm, tn), lambda i, j: (i, j)),
        out_shape=jax.ShapeDtypeStruct((lp, n), out_dtype),
        scratch_shapes=[pltpu.VMEM((tm, D_MODEL), BF16)],
        compiler_params=_cparams(("parallel", "arbitrary")),
        name="proj",
    )(h, w, col_scale)


def _outproj_ln_body(x_ref, y_ref, w_ref, g_ref, b_ref, o_ref, ob_ref):
    acc = jnp.dot(y_ref[...], w_ref[...], preferred_element_type=F32)
    out = _ln(ALPHA * x_ref[...] + acc, g_ref[...], b_ref[...])
    o_ref[...] = out
    ob_ref[...] = out.astype(BF16)


def outproj_ln(h, y, w, g, b):
    lp, k = y.shape
    tm = OUT_TM
    return pl.pallas_call(
        _outproj_ln_body,
        grid=(lp // tm,),
        in_specs=[pl.BlockSpec((tm, D_MODEL), lambda i: (i, 0)),
                  pl.BlockSpec((tm, k), lambda i: (i, 0)),
                  pl.BlockSpec((k, D_MODEL), lambda i: (0, 0)),
                  pl.BlockSpec((1, D_MODEL), lambda i: (0, 0)),
                  pl.BlockSpec((1, D_MODEL), lambda i: (0, 0))],
        out_specs=[pl.BlockSpec((tm, D_MODEL), lambda i: (i, 0))] * 2,
        out_shape=[jax.ShapeDtypeStruct((lp, D_MODEL), F32), jax.ShapeDtypeStruct((lp, D_MODEL), BF16)],
        compiler_params=_cparams(("parallel",)),
        name="outproj_ln",
    )(h, y, w, g, b)


def _rwkv_prep_body(x_ref, xp_ref, mix_ref, wrkv_ref, w1_ref, a1_ref, g1_ref, w2_ref, a2_ref, g2_ref,
                    w0_ref, a0_ref, kk_ref, ka_ref, bd_ref,
                    r_o, lw_o, k_o, v_o, kk_o, b_o, g_o,
                    xx_ref, xs_ref, tw_ref, ta_ref, tg_ref):
    i, j = pl.program_id(0), pl.program_id(1)
    tm = x_ref.shape[0]

    @pl.when(j == 0)
    def _():
        x = x_ref[...]
        prev = jnp.where(i == 0, 0.0, xp_ref[SUBLANE - 1:SUBLANE, :])
        first = lax.broadcasted_iota(jnp.int32, x.shape, 0) == 0
        xx_ref[...] = jnp.where(first, prev, pltpu.roll(x, 1, 0)) - x
        for s in range(3):
            xs_ref[s] = (x + xx_ref[...] * mix_ref[s:s + 1, :]).astype(BF16)
        xs_w = (x + xx_ref[...] * mix_ref[3:4, :]).astype(BF16)
        xs_a = (x + xx_ref[...] * mix_ref[4:5, :]).astype(BF16)
        xs_g = (x + xx_ref[...] * mix_ref[5:6, :]).astype(BF16)
        tw_ref[...] = jnp.tanh(jnp.dot(xs_w, w1_ref[...], preferred_element_type=F32)).astype(BF16)
        ta_ref[...] = jnp.dot(xs_a, a1_ref[...], preferred_element_type=F32).astype(BF16)
        tg_ref[...] = jax.nn.sigmoid(jnp.dot(xs_g, g1_ref[...], preferred_element_type=F32)).astype(BF16)

    r = jnp.dot(xs_ref[0], wrkv_ref[0], preferred_element_type=F32)
    k = jnp.dot(xs_ref[1], wrkv_ref[1], preferred_element_type=F32)
    v = jnp.dot(xs_ref[2], wrkv_ref[2], preferred_element_type=F32)
    wz = w0_ref[...] + jnp.dot(tw_ref[...], w2_ref[...], preferred_element_type=F32)
    az = a0_ref[...] + jnp.dot(ta_ref[...], a2_ref[...], preferred_element_type=F32)
    g = jnp.dot(tg_ref[...], g2_ref[...], preferred_element_type=F32)

    w_log = -_softplus(-wz) - 0.5
    a = jax.nn.sigmoid(az)
    kk = k * kk_ref[...]
    kk2 = kk * kk
    bd = bd_ref[...]
    kk2_hi = kk2.astype(BF16)
    kk2_lo = (kk2 - kk2_hi.astype(F32)).astype(BF16)
    ss = jnp.concatenate(
        [jnp.dot(jnp.concatenate([kk2_hi[:, c * LANE:(c + 1) * LANE], kk2_lo[:, c * LANE:(c + 1) * LANE]], axis=1),
                 bd, preferred_element_type=F32)
         for c in range(kk2.shape[1] // LANE)], axis=1)
    kkn = kk * lax.rsqrt(ss + 1e-12)

    r_o[...] = r
    lw_o[...] = -jnp.exp(w_log)
    k_o[...] = k * (1.0 + (a - 1.0) * ka_ref[...])
    v_o[...] = v
    kk_o[...] = kkn
    b_o[...] = kkn * a
    g_o[...] = g


def rwkv_prep(h, p):
    lp = h.shape[0]
    tm, tn = RWKV_PREP_TM, RWKV_PREP_TN
    row = lambda i, j: (i, 0)
    col = lambda i, j: (0, j)
    full = lambda i, j: (0, 0)
    lw, la, lg = p["w1"].shape[1], p["a1"].shape[1], p["g1"].shape[1]
    out = jax.ShapeDtypeStruct((lp, D_MODEL), F32)
    return pl.pallas_call(
        _rwkv_prep_body,
        grid=(lp // tm, D_MODEL // tn),
        in_specs=[
            pl.BlockSpec((tm, D_MODEL), row),
            pl.BlockSpec((SUBLANE, D_MODEL), lambda i, j: (jnp.maximum(i * (tm // SUBLANE) - 1, 0), 0)),
            pl.BlockSpec((SUBLANE, D_MODEL), full),
            pl.BlockSpec((3, D_MODEL, tn), lambda i, j: (0, 0, j)),
            pl.BlockSpec((D_MODEL, lw), full),
            pl.BlockSpec((D_MODEL, la), full),
            pl.BlockSpec((D_MODEL, lg), full),
            pl.BlockSpec((lw, tn), col),
            pl.BlockSpec((la, tn), col),
            pl.BlockSpec((lg, tn), col),
            pl.BlockSpec((1, tn), col),
            pl.BlockSpec((1, tn), col),
            pl.BlockSpec((1, tn), col),
            pl.BlockSpec((1, tn), col),
            pl.BlockSpec((2 * LANE, LANE), full),
        ],
        out_specs=[pl.BlockSpec((tm, tn), lambda i, j: (i, j))] * 7,
        out_shape=[out] * 7,
        scratch_shapes=[pltpu.VMEM((tm, D_MODEL), F32), pltpu.VMEM((3, tm, D_MODEL), BF16),
                        pltpu.VMEM((tm, lw), BF16), pltpu.VMEM((tm, la), BF16), pltpu.VMEM((tm, lg), BF16)],
        compiler_params=_cparams(("parallel", "arbitrary")),
        name="rwkv_prep",
    )(h, h, p["mix"], p["w_rkv"], p["w1"], p["a1"], p["g1"], p["w2"], p["a2"], p["g2"],
      p["w0"], p["a0"], p["k_k"], p["k_a"], p["bd"])


RWKV_GROUP = 4
RWKV_GW = RWKV_GROUP * RWKV_HEAD_DIM
RWKV_SOLVE_BASE = 8

_NN = ((1,), (0,))
_NT = ((1,), (1,))
_TN = ((0,), (0,))


def _mm(a, b, dims):
    return lax.dot_general(a.astype(BF16), b.astype(BF16), (dims, ((), ())), preferred_element_type=F32)


def _rwkv_chunk_body(r_ref, lw_ref, k_ref, v_ref, kk_ref, b_ref, g_ref, rk_ref, gng_ref, gnb_ref, tri_ref,
                     o_ref, s_ref):
    c = r_ref.shape[0]
    n = RWKV_HEAD_DIM
    gw = RWKV_GW

    @pl.when(pl.program_id(0) == 0)
    def _():
        s_ref[...] = jnp.zeros_like(s_ref)

    lw = lw_ref[...]
    cum = jnp.dot(tri_ref[...], lw, precision=HI, preferred_element_type=F32)
    e_pos = jnp.exp(cum)
    e_neg = jnp.exp(-cum)
    e_end = e_pos[c - 1:c, :]
    at_all = -kk_ref[...] * jnp.exp(cum - lw)
    rt_all = r_ref[...] * e_pos
    bb_all = b_ref[...] * e_neg
    kb_all = k_ref[...] * e_neg
    rkr_all = r_ref[...] * k_ref[...] * rk_ref[...]

    blk = (lax.broadcasted_iota(jnp.int32, (gw, gw), 0) // n == lax.broadcasted_iota(jnp.int32, (gw, gw), 1) // n)
    blk = blk.astype(F32).astype(BF16)
    row = lax.broadcasted_iota(jnp.int32, (c, gw), 0)
    src = lax.broadcasted_iota(jnp.int32, (c, gw), 1) % n
    strict = src < row
    incl = src <= row

    def bdiag(x):
        return jnp.tile(x.astype(BF16), (RWKV_GROUP, 1)) * blk

    def seg_sums(xs):
        hi = [x.astype(BF16) for x in xs]
        lo = [(x - h_.astype(F32)).astype(BF16) for x, h_ in zip(xs, hi)]
        tot = jnp.dot(jnp.concatenate(hi + lo, axis=0), blk, preferred_element_type=F32)
        rows = xs[0].shape[0]
        return [tot[i * rows:(i + 1) * rows] + tot[(len(xs) + i) * rows:(len(xs) + i + 1) * rows]
                for i in range(len(xs))]

    groups = [slice(g * gw, (g + 1) * gw) for g in range(D_MODEL // gw)]
    v_g = [v_ref[:, sl] for sl in groups]
    lhs = [jnp.concatenate([at_all[:, sl], rt_all[:, sl]], axis=0).astype(BF16) for sl in groups]
    g_b = [_mm(a, bdiag(bb_all[:, sl]), _NT) for a, sl in zip(lhs, groups)]
    g_k = [_mm(a, bdiag(kb_all[:, sl]), _NT) for a, sl in zip(lhs, groups)]
    s0 = [s_ref[g] for g in range(len(groups))]
    z = [_mm(a, bdiag(s_), _NT) for a, s_ in zip(lhs, s0)]
    a_ab = [jnp.where(strict, gm[:c], 0.0) for gm in g_b]
    a_ak = [jnp.where(strict, gm[:c], 0.0) for gm in g_k]
    a_r = [jnp.concatenate([jnp.where(incl, gb[c:], 0.0), jnp.where(incl, gk[c:], 0.0)], axis=1).astype(BF16)
           for gb, gk in zip(g_b, g_k)]
    bd_v = [bdiag(v_) for v_ in v_g]
    rhs = [z_[:c] + _mm(a, bv, _NN) for z_, a, bv in zip(z, a_ak, bd_v)]
    def within(size):
        return (row // size) == (src // size)

    tinv = [jnp.where(within(RWKV_SOLVE_BASE), a, 0.0) for a in a_ab]
    pw = tinv
    tinv = [(src == row).astype(F32) + t_ for t_ in tinv]
    span = 2
    while span < RWKV_SOLVE_BASE:
        pw = [_mm(p_, bdiag(p_), _NN) for p_ in pw]
        tinv = [t_ + _mm(p_, bdiag(t_), _NN) for t_, p_ in zip(tinv, pw)]
        span *= 2
    size = RWKV_SOLVE_BASE
    while size < c:
        link = within(2 * size) & jnp.logical_not(within(size))
        bd_t = [bdiag(t_) for t_ in tinv]
        tl = [_mm(t_, bdiag(jnp.where(link, a, 0.0)), _NN) for t_, a in zip(tinv, a_ab)]
        tinv = [t_ + _mm(x_, b_, _NN) for t_, x_, b_ in zip(tinv, tl, bd_t)]
        size *= 2
    u = [_mm(t_, bdiag(r_), _NN) for t_, r_ in zip(tinv, rhs)]
    y = [z_[c:] + _mm(a, jnp.concatenate([bdiag(u_), bv], axis=0), _NN)
         for z_, a, u_, bv in zip(z, a_r, u, bd_v)]
    for g, sl in enumerate(groups):
        e_end_g = e_end[:, sl]
        uv = jnp.concatenate([u[g], v_g[g]], axis=0)
        bk = jnp.concatenate([bb_all[:, sl], kb_all[:, sl]], axis=0) * e_end_g
        full = _mm(uv, bk, _TN) * blk.astype(F32)
        s_ref[g] = s0[g] * e_end_g + functools.reduce(
            lambda p_, q_: p_ + q_, [full[q * n:(q + 1) * n, :] for q in range(RWKV_GROUP)])
    sums = seg_sums(y + [rkr_all[:, sl] for sl in groups])
    yc = [y_ - m_ * (1.0 / n) for y_, m_ in zip(y, sums[:len(groups)])]
    var = seg_sums([t * t for t in yc])
    for g, sl in enumerate(groups):
        yn = yc[g] * lax.rsqrt(var[g] * (1.0 / n) + RWKV_GN_EPS) * gng_ref[:, sl] + gnb_ref[:, sl]
        bonus = sums[len(groups) + g] * v_g[g]
        o_ref[:, sl] = ((yn + bonus) * g_ref[:, sl]).astype(o_ref.dtype)


def rwkv_chunk(r, lw, k, v, kk, b, g, r_k, gn_g, gn_b):
    lp = r.shape[0]
    c = RWKV_CHUNK
    tri = jnp.asarray(np.tril(np.ones((c, c), np.float32)))
    blk = pl.BlockSpec((c, D_MODEL), lambda i: (i, 0))
    vec = pl.BlockSpec((1, D_MODEL), lambda i: (0, 0))
    return pl.pallas_call(
        _rwkv_chunk_body,
        grid=(lp // c,),
        in_specs=[blk] * 7 + [vec] * 3 + [pl.BlockSpec((c, c), lambda i: (0, 0))],
        out_specs=blk,
        out_shape=jax.ShapeDtypeStruct((lp, D_MODEL), BF16),
        scratch_shapes=[pltpu.VMEM((D_MODEL // RWKV_GW, RWKV_HEAD_DIM, RWKV_GW), F32)],
        compiler_params=_cparams(("arbitrary",)),
        name="rwkv_chunk",
    )(r, lw, k, v, kk, b, g, r_k, gn_g, gn_b, tri)


def _pad_rows(a, rows):
    return jnp.pad(a, ((0, rows - a.shape[0]), (0, 0)))


def _pad_cols(a, cols):
    return jnp.pad(a, ((0, 0), (0, cols - a.shape[1])))


def rwkv_params(mix, w_rkv, w0, w1, w2, a0, a1, a2, g1, g2, k_k, k_a, r_k, gn_g, gn_b, w_o):
    lora = LANE
    bd = np.kron(np.eye(LANE // RWKV_HEAD_DIM, dtype=np.float32),
                 np.ones((RWKV_HEAD_DIM, RWKV_HEAD_DIM), np.float32))
    bd = np.concatenate([bd, bd], axis=0)
    row = lambda t: t.reshape(1, D_MODEL)
    return dict(
        mix=_pad_rows(mix, SUBLANE), w_rkv=w_rkv.astype(BF16),
        w1=_pad_cols(w1, lora).astype(BF16), w2=_pad_rows(w2, lora).astype(BF16),
        a1=_pad_cols(a1, lora).astype(BF16), a2=_pad_rows(a2, lora).astype(BF16),
        g1=g1.astype(BF16), g2=g2.astype(BF16),
        w0=row(w0), a0=row(a0), k_k=row(k_k), k_a=row(k_a), bd=jnp.asarray(bd, BF16),
        r_k=row(r_k), gn_g=row(gn_g), gn_b=row(gn_b), w_o=w_o.astype(BF16))


def rwkv_mixer(h, p):
    r, lw, k, v, kk, b, g = rwkv_prep(h, p)
    return rwkv_chunk(r, lw, k, v, kk, b, g, p["r_k"], p["gn_g"], p["gn_b"])


def _rg_windows():
    starts = []
    for c in range(D_RNN // LANE):
        first_block = (c * LANE) // RG_BLOCK_DIM
        ks = (first_block * RG_BLOCK_DIM) // LANE * LANE
        starts.append(min(ks, D_RNN - RG_KWIN))
    return starts


def _rg_compact(w):
    eye = jnp.eye(RG_BLOCKS, dtype=w.dtype)
    dense = (eye[:, None, :, None] * w[:, :, None, :]).reshape(D_RNN, D_RNN)
    tiles = [dense[ks:ks + RG_KWIN, c * LANE:(c + 1) * LANE] for c, ks in enumerate(_rg_windows())]
    return jnp.stack(tiles).astype(BF16)


def _gelu_tanh(x):
    return 0.5 * x * (1.0 + jnp.tanh(np.sqrt(2.0 / np.pi).astype(np.float32) * (x + 0.044715 * (x * x * x))))


def _rg_body(gate_ref, u_ref, up_ref, cw_ref, cb_ref, wa_ref, wx_ref, ba_ref, bx_ref, lam_ref, o_ref,
             ue_ref, carry_ref):
    i = pl.program_id(0)
    tm = u_ref.shape[0]

    @pl.when(i == 0)
    def _():
        carry_ref[...] = jnp.zeros_like(carry_ref)

    u = u_ref[...]
    ue_ref[0:SUBLANE, :] = jnp.where(i == 0, 0.0, up_ref[...])
    ue_ref[SUBLANE:, :] = u
    uc = cb_ref[...] + u * cw_ref[CONV_W - 1:CONV_W, :]
    for d in range(1, CONV_W):
        uc = uc + ue_ref[pl.ds(SUBLANE - d, tm), :] * cw_ref[CONV_W - 1 - d:CONV_W - d, :]
    ucb = uc.astype(BF16)
    za, zx = [], []
    for c, ks in enumerate(_rg_windows()):
        win = ucb[:, ks:ks + RG_KWIN]
        za.append(jnp.dot(win, wa_ref[c], preferred_element_type=F32))
        zx.append(jnp.dot(win, wx_ref[c], preferred_element_type=F32))
    r = jax.nn.sigmoid(jnp.concatenate(za, axis=1) + ba_ref[...])
    ig = jax.nn.sigmoid(jnp.concatenate(zx, axis=1) + bx_ref[...])
    log_a = -RG_C * r * _softplus(-lam_ref[...])
    a = jnp.exp(log_a)
    w2 = -2.0 * jnp.tanh(log_a)
    q = w2 * (1.0 + 0.5 * w2)
    b = jnp.where(q > 0.0, w2 * lax.rsqrt(q), 0.0) * (ig * uc)
    groups = tm // SUBLANE
    a = a.reshape(groups, SUBLANE, D_RNN)
    b = b.reshape(groups, SUBLANE, D_RNN)
    sub = lax.broadcasted_iota(jnp.int32, a.shape, 1)
    s = 1
    while s < SUBLANE:
        keep = sub >= s
        b = jnp.where(keep, a * pltpu.roll(b, s, 1) + b, b)
        a = jnp.where(keep, a * pltpu.roll(a, s, 1), a)
        s *= 2
    carry = carry_ref[0:1, :]
    for g in range(groups):
        hg = a[g] * carry + b[g]
        rows = pl.ds(g * SUBLANE, SUBLANE)
        o_ref[rows, :] = (_gelu_tanh(gate_ref[rows, :]) * hg).astype(o_ref.dtype)
        carry = hg[SUBLANE - 1:SUBLANE, :]
    carry_ref[...] = jnp.broadcast_to(carry, carry_ref.shape)


def rg_params(w_in, conv_w, conv_b, w_a, b_a, w_x, b_x, lam, w_o):
    row = lambda t: t.reshape(1, D_RNN)
    return dict(w_in=w_in.astype(BF16), conv_w=_pad_rows(conv_w, SUBLANE), conv_b=row(conv_b),
                w_a=_rg_compact(w_a), w_x=_rg_compact(w_x), b_a=row(b_a), b_x=row(b_x),
                lam=row(lam), w_o=w_o.astype(BF16), ones=jnp.ones((1, 2 * D_RNN), F32))


def rg_mixer(h, p):
    lp = h.shape[0]
    tm = RG_TM
    gu = proj(h, p["w_in"], p["ones"], 768, F32)
    ntile = D_RNN // LANE
    vec = pl.BlockSpec((1, D_RNN), lambda i: (0, 0))
    wspec = pl.BlockSpec((ntile, RG_KWIN, LANE), lambda i: (0, 0, 0))
    return pl.pallas_call(
        _rg_body,
        grid=(lp // tm,),
        in_specs=[pl.BlockSpec((tm, D_RNN), lambda i: (i, 0)),
                  pl.BlockSpec((tm, D_RNN), lambda i: (i, 1)),
                  pl.BlockSpec((SUBLANE, D_RNN), lambda i: (jnp.maximum(i * (tm // SUBLANE) - 1, 0), 1)),
                  pl.BlockSpec((SUBLANE, D_RNN), lambda i: (0, 0)),
                  vec, wspec, wspec, vec, vec, vec],
        out_specs=pl.BlockSpec((tm, D_RNN), lambda i: (i, 0)),
        out_shape=jax.ShapeDtypeStruct((lp, D_RNN), BF16),
        scratch_shapes=[pltpu.VMEM((tm + SUBLANE, D_RNN), F32), pltpu.VMEM((SUBLANE, D_RNN), F32)],
        compiler_params=_cparams(("arbitrary",)),
        name="rg_lru",
    )(gu, gu, gu, p["conv_w"], p["conv_b"], p["w_a"], p["w_x"], p["b_a"], p["b_x"], p["lam"])


def _split3(x):
    hi = x.astype(BF16).astype(F32)
    mid = (x - hi).astype(BF16).astype(F32)
    lo = (x - hi - mid).astype(BF16).astype(F32)
    return hi, mid, lo


def _fox_gate_body(x_ref, w_ref, bf_ref, cc_ref, qx_ref, kx_ref, carry_ref):
    @pl.when(pl.program_id(0) == 0)
    def _():
        carry_ref[...] = jnp.zeros_like(carry_ref)

    tm = x_ref.shape[0]
    fl = jnp.dot(x_ref[...].astype(BF16), w_ref[...], preferred_element_type=F32)
    c = _scan_add(_log_sigmoid(fl + bf_ref[...]), 0) + carry_ref[0:1, :]
    carry_ref[...] = jnp.broadcast_to(c[tm - 1:, :], carry_ref.shape)
    c = c * LOG2E
    cc_ref[...] = jnp.broadcast_to(c[0:1, :], cc_ref.shape)
    fine = c - c[0:1, :]
    lane = lax.broadcasted_iota(jnp.int32, (tm, LANE), 1)
    ones_mid = jnp.where(lane < 6, 1.0, 0.0)
    for h in range(FOX_HEADS):
        hi, mid, lo = [jnp.broadcast_to(t, (tm, LANE)) for t in _split3(fine[:, h:h + 1])]
        qx = jnp.where(lane == 0, hi, jnp.where(lane == 1, mid, lo))
        kx = -jnp.where(lane == 3, hi, jnp.where(lane == 4, mid, lo))
        qx_ref[:, h * LANE:(h + 1) * LANE] = jnp.where(lane < 3, qx, ones_mid).astype(BF16)
        kx_ref[:, h * LANE:(h + 1) * LANE] = jnp.where(lane < 3, 1.0, jnp.where(lane < 6, kx, 0.0)).astype(BF16)


def fox_gate(h, w_f, b_f):
    lp = h.shape[0]
    tm = FOX_TQ
    nt = lp // tm
    wide = jax.ShapeDtypeStruct((lp, FOX_HEADS * LANE), BF16)
    return pl.pallas_call(
        _fox_gate_body,
        grid=(nt,),
        in_specs=[pl.BlockSpec((tm, D_MODEL), lambda i: (i, 0)),
                  pl.BlockSpec((D_MODEL, LANE), lambda i: (0, 0)),
                  pl.BlockSpec((1, LANE), lambda i: (0, 0))],
        out_specs=[pl.BlockSpec((SUBLANE, LANE), lambda i: (i, 0)),
                   pl.BlockSpec((tm, FOX_HEADS * LANE), lambda i: (i, 0)),
                   pl.BlockSpec((tm, FOX_HEADS * LANE), lambda i: (i, 0))],
        out_shape=[jax.ShapeDtypeStruct((nt * SUBLANE, LANE), F32), wide, wide],
        scratch_shapes=[pltpu.VMEM((SUBLANE, LANE), F32)],
        compiler_params=_cparams(("arbitrary",)),
        name="fox_gate",
    )(h, w_f, b_f)


def _fox_attn_body(cc_ref, q_ref, qx_ref, k_ref, kx_ref, v_ref, o_ref,
                   s_ref, p_ref, mx_ref, d_ref, m_ref, corr_ref, acc_ref):
    h, i = pl.program_id(0), pl.program_id(1)
    tq = q_ref.shape[0]
    nl = tq // LANE
    q_aug = jnp.concatenate([q_ref[...], qx_ref[...]], axis=1)
    ones_col = (lax.broadcasted_iota(jnp.int32, (tq, LANE), 1) == 0).astype(BF16)
    strips = list(range(0, tq, FOX_STRIP))

    def tile_rows(j):
        return pl.ds(pl.multiple_of(j * tq, tq), tq)

    def logits(j, slot):
        rows = tile_rows(j)
        k_aug = jnp.concatenate([k_ref[rows, :], kx_ref[rows, :]], axis=1)
        s_ref[slot] = lax.dot_general(q_aug, k_aug, (_NT, ((), ())), preferred_element_type=F32)

    def softmax(j, slot, diagonal):
        delta = cc_ref[h, i] - cc_ref[h, j]
        for r0 in strips:
            rows = pl.ds(r0, FOX_STRIP)
            t = s_ref[slot, rows, :]
            if diagonal:
                row = r0 + lax.broadcasted_iota(jnp.int32, t.shape, 0)
                col = lax.broadcasted_iota(jnp.int32, t.shape, 1)
                t = jnp.where(col <= row, t, -jnp.inf)
                s_ref[slot, rows, :] = t
            mx_ref[rows, :] = functools.reduce(jnp.maximum, [t[:, c * LANE:(c + 1) * LANE] for c in range(nl)])
        m_old = m_ref[...]
        m_new = jnp.maximum(m_old, jnp.max(mx_ref[...], -1, keepdims=True) + delta)
        m_ref[...] = m_new
        corr_ref[slot] = jnp.exp2(m_old - m_new)
        d_ref[...] = jnp.broadcast_to(delta - m_new, d_ref.shape)
        for r0 in strips:
            rows = pl.ds(r0, FOX_STRIP)
            e = s_ref[slot, rows, :] + jnp.tile(d_ref[rows, :], (1, nl))
            p_ref[slot, rows, :] = jnp.exp2(e).astype(BF16)

    def accumulate(j, slot):
        vb = jnp.concatenate([v_ref[tile_rows(j), :], ones_col], axis=1)
        acc_ref[...] = corr_ref[slot] * acc_ref[...] + jnp.dot(p_ref[slot], vb, preferred_element_type=F32)

    m_ref[...] = jnp.full(m_ref.shape, -jnp.inf, F32)
    acc_ref[...] = jnp.zeros_like(acc_ref)
    p_ref[1] = jnp.zeros(p_ref.shape[1:], BF16)
    corr_ref[1] = jnp.ones(corr_ref.shape[1:], F32)
    logits(0, 0)

    def pair(a, carry):
        j = 2 * a
        accumulate(jnp.maximum(j - 1, 0), 1)
        softmax(j, 0, False)
        logits(j + 1, 1)
        accumulate(j, 0)
        softmax(j + 1, 1, False)
        logits(j + 2, 0)
        return carry

    lax.fori_loop(0, i // 2, pair, 0)

    @pl.when(i % 2 == 0)
    def _():
        accumulate(jnp.maximum(i - 1, 0), 1)
        softmax(i, 0, True)
        accumulate(i, 0)

    @pl.when(i % 2 == 1)
    def _():
        accumulate(jnp.maximum(i - 2, 0), 1)
        softmax(i - 1, 0, False)
        logits(i, 1)
        accumulate(i - 1, 0)
        softmax(i, 1, True)
        accumulate(i, 1)

    acc = acc_ref[...]
    o_ref[...] = (acc[:, :FOX_HEAD_DIM] / acc[:, FOX_HEAD_DIM:FOX_HEAD_DIM + 1]).astype(o_ref.dtype)


def fox_attn(qkv, qx, kx, cc):
    lp = qkv.shape[0]
    tq = FOX_TQ
    hh = FOX_HEADS
    hd = FOX_HEAD_DIM
    return pl.pallas_call(
        _fox_attn_body,
        grid=(hh, lp // tq),
        in_specs=[pl.BlockSpec(memory_space=pltpu.SMEM),
                  pl.BlockSpec((tq, hd), lambda h, i: (i, h)),
                  pl.BlockSpec((tq, LANE), lambda h, i: (i, h)),
                  pl.BlockSpec((lp, hd), lambda h, i: (0, hh + h)),
                  pl.BlockSpec((lp, LANE), lambda h, i: (0, h)),
                  pl.BlockSpec((lp, hd), lambda h, i: (0, 2 * hh + h))],
        out_specs=pl.BlockSpec((tq, hd), lambda h, i: (i, h)),
        out_shape=jax.ShapeDtypeStruct((lp, D_MODEL), BF16),
        scratch_shapes=[pltpu.VMEM((2, tq, tq), F32), pltpu.VMEM((2, tq, tq), BF16), pltpu.VMEM((tq, LANE), F32),
                        pltpu.VMEM((tq, LANE), F32), pltpu.VMEM((tq, 1), F32), pltpu.VMEM((2, tq, 1), F32),
                        pltpu.VMEM((tq, hd + LANE), F32)],
        compiler_params=_cparams(("parallel", "arbitrary")),
        name="fox_attn",
    )(cc, qkv, qx, qkv, kx, qkv)


def fox_params(w_in, b_f, w_o):
    q_scale = jnp.full((1, D_MODEL), LOG2E * FOX_HEAD_DIM ** -0.5, F32)
    return dict(w_qkv=w_in[:, :3 * D_MODEL].astype(BF16),
                qkv_scale=jnp.concatenate([q_scale, jnp.ones((1, 2 * D_MODEL), F32)], axis=1),
                w_f=_pad_cols(w_in[:, 3 * D_MODEL:], LANE).astype(BF16),
                b_f=_pad_cols(b_f.reshape(1, FOX_HEADS), LANE), w_o=w_o.astype(BF16))


def fox_mixer(h, p):
    qkv = proj(h, p["w_qkv"], p["qkv_scale"], 768, BF16)
    cc, qx, kx = fox_gate(h, p["w_f"], p["b_f"])
    cc = cc[::SUBLANE, :FOX_HEADS].T
    return fox_attn(qkv, qx, kx, cc)


def _mlstm_body(x_ref, gc_ref, gr_ref, bc_ref, br_ref, ng_ref, o_ref, c_ref, n_ref, m_ref):
    cm = x_ref.shape[0]
    hh = ML_HEADS

    @pl.when(pl.program_id(0) == 0)
    def _():
        c_ref[...] = jnp.zeros_like(c_ref)
        n_ref[...] = jnp.zeros_like(n_ref)
        m_ref[...] = jnp.zeros_like(m_ref)

    gc = gc_ref[...] + bc_ref[...]
    bcum_c = _scan_add(_log_sigmoid(gc), 0)
    gr = gr_ref[...] + br_ref[:, 0:1]
    bcum_r = _scan_add(_log_sigmoid(gr), 1)
    row = lax.broadcasted_iota(jnp.int32, (cm, cm), 0)
    col = lax.broadcasted_iota(jnp.int32, (cm, cm), 1)
    causal = col <= row

    for h in range(hh):
        b_col = bcum_c[:, hh + h:hh + h + 1]
        ig_col = gc[:, h:h + 1]
        b_row = bcum_r[hh + h:hh + h + 1, :]
        ig_row = gr[h:h + 1, :]
        m_prev = m_ref[h][0:1, 0:1]
        q = x_ref[:, h * ML_DQK:(h + 1) * ML_DQK].astype(BF16)
        k = x_ref[:, ML_QK + h * ML_DQK:ML_QK + (h + 1) * ML_DQK] * (ML_DQK ** -0.5)
        v = x_ref[:, 2 * ML_QK + h * ML_DV:2 * ML_QK + (h + 1) * ML_DV].astype(BF16)
        og = x_ref[:, 2 * ML_QK + ML_V + h * ML_DV:2 * ML_QK + ML_V + (h + 1) * ML_DV]

        dmat = jnp.where(causal, b_col - b_row + ig_row, -jnp.inf)
        inter = b_col + m_prev
        m_t = jnp.maximum(inter, jnp.max(dmat, -1, keepdims=True))
        w_intra = jnp.exp(dmat - m_t)
        w_inter = jnp.exp(inter - m_t)
        s = lax.dot_general(q, k.astype(BF16), (_NT, ((), ())), preferred_element_type=F32) * w_intra
        c_st = c_ref[h]
        n_st = n_ref[h][0:1, :]
        num = (jnp.dot(s.astype(BF16), v, preferred_element_type=F32)
               + w_inter * jnp.dot(q, c_st.astype(BF16), preferred_element_type=F32))
        den = jnp.sum(s, -1, keepdims=True) + w_inter * jnp.sum(q.astype(F32) * n_st, -1, keepdims=True)
        hv = num / jnp.maximum(jnp.abs(den), jnp.exp(-m_t))

        b_end = b_col[cm - 1:cm, :]
        d_end = b_end - b_col + ig_col
        m_new = jnp.maximum(b_end + m_prev, jnp.max(d_end, 0, keepdims=True))
        kw = k * jnp.exp(d_end - m_new)
        carry = jnp.exp(b_end + m_prev - m_new)
        c_ref[h] = carry * c_st + lax.dot_general(kw.astype(BF16), v, (_TN, ((), ())), preferred_element_type=F32)
        n_ref[h] = jnp.broadcast_to(carry * n_st + jnp.sum(kw, 0, keepdims=True), n_ref.shape[1:])
        m_ref[h] = jnp.broadcast_to(m_new, m_ref.shape[1:])

        hn = hv * lax.rsqrt(jnp.mean(hv * hv, -1, keepdims=True) + ML_EPS)
        sl = slice(h * ML_DV, (h + 1) * ML_DV)
        o_ref[:, sl] = (hn * ng_ref[:, sl] * jax.nn.sigmoid(og)).astype(o_ref.dtype)


def mlstm_params(w_in, b_if, norm_g, w_o):
    nmain = 2 * ML_QK + 2 * ML_V
    b_c = _pad_cols(b_if.reshape(1, 2 * ML_HEADS), LANE)
    b_r = jnp.broadcast_to(b_if.reshape(2 * ML_HEADS, 1), (2 * ML_HEADS, LANE))
    return dict(w_main=w_in[:, :nmain].astype(BF16), w_g=_pad_cols(w_in[:, nmain:], LANE).astype(BF16),
                ones=jnp.ones((1, nmain), F32), b_c=b_c, b_r=b_r, norm_g=norm_g.reshape(1, ML_V),
                w_o=w_o.astype(BF16))


def mlstm_mixer(h, p):
    lp = h.shape[0]
    cm = ML_CHUNK
    nmain = 2 * ML_QK + 2 * ML_V
    main = proj(h, p["w_main"], p["ones"], 768, F32)
    gates = proj(h, p["w_g"], p["ones"][:, :LANE], LANE, F32)
    gates_r = gates[:, :2 * ML_HEADS].T
    return pl.pallas_call(
        _mlstm_body,
        grid=(lp // cm,),
        in_specs=[pl.BlockSpec((cm, nmain), lambda i: (i, 0)),
                  pl.BlockSpec((cm, LANE), lambda i: (i, 0)),
                  pl.BlockSpec((2 * ML_HEADS, cm), lambda i: (0, i)),
                  pl.BlockSpec((1, LANE), lambda i: (0, 0)),
                  pl.BlockSpec((2 * ML_HEADS, LANE), lambda i: (0, 0)),
                  pl.BlockSpec((1, ML_V), lambda i: (0, 0))],
        out_specs=pl.BlockSpec((cm, ML_V), lambda i: (i, 0)),
        out_shape=jax.ShapeDtypeStruct((lp, ML_V), BF16),
        scratch_shapes=[pltpu.VMEM((ML_HEADS, ML_DQK, ML_DV), F32),
                        pltpu.VMEM((ML_HEADS, SUBLANE, ML_DQK), F32),
                        pltpu.VMEM((ML_HEADS, SUBLANE, LANE), F32)],
        compiler_params=_cparams(("arbitrary",)),
        name="mlstm",
    )(main, gates, gates_r, p["b_c"], p["b_r"], p["norm_g"])


def kernel(x, meta_tokens, ln_g, ln_b, ffn_in, ffn_out, rwkv_mix, rwkv_w_rkv, rwkv_w0, rwkv_w1, rwkv_w2, rwkv_a0, rwkv_a1, rwkv_a2, rwkv_g1, rwkv_g2, rwkv_k_k, rwkv_k_a, rwkv_r_k, rwkv_gn_g, rwkv_gn_b, rwkv_w_o, rg_w_in, rg_conv_w, rg_conv_b, rg_w_a, rg_b_a, rg_w_x, rg_b_x, rg_lambda, rg_w_o, fox_w_in, fox_b_f, fox_w_o, ml_w_in, ml_b_if, ml_norm_g, ml_w_o):
    batch, seq, _ = x.shape
    assert batch == 1
    l = seq + N_META
    lp = -(-l // ROW_ALIGN) * ROW_ALIGN
    h = jnp.concatenate([meta_tokens.astype(x.dtype), x[0], jnp.zeros((lp - l, D_MODEL), x.dtype)], axis=0)

    ffn_in_b = ffn_in.astype(BF16)
    ffn_out_b = ffn_out.astype(BF16)
    rw = rwkv_params(rwkv_mix, rwkv_w_rkv, rwkv_w0, rwkv_w1, rwkv_w2, rwkv_a0, rwkv_a1, rwkv_a2, rwkv_g1,
                     rwkv_g2, rwkv_k_k, rwkv_k_a, rwkv_r_k, rwkv_gn_g, rwkv_gn_b, rwkv_w_o)
    rg = rg_params(rg_w_in, rg_conv_w, rg_conv_b, rg_w_a, rg_b_a, rg_w_x, rg_b_x, rg_lambda, rg_w_o)
    fx = fox_params(fox_w_in, fox_b_f, fox_w_o)
    ml = mlstm_params(ml_w_in, ml_b_if, ml_norm_g, ml_w_o)
    mixers = ((rwkv_mixer, rw), (rg_mixer, rg), (fox_mixer, fx), (mlstm_mixer, ml))

    vec = lambda t: t.reshape(1, D_MODEL)
    hb = h.astype(BF16)
    for layer in range(DEPTH):
        h, hb = ffn_ln(h, hb, ffn_in_b, ffn_out_b, layer, 0, vec(ln_g[layer, 0]), vec(ln_b[layer, 0]))
        mixer, p = mixers[layer % len(mixers)]
        y = mixer(h, p)
        h, hb = outproj_ln(h, y, p["w_o"], vec(ln_g[layer, 1]), vec(ln_b[layer, 1]))
        h, hb = ffn_ln(h, hb, ffn_in_b, ffn_out_b, layer, 1, vec(ln_g[layer, 2]), vec(ln_b[layer, 2]))
    return h[N_META:l][None]
```

```python
import functools

import numpy as np
import jax
import jax.numpy as jnp
from jax import lax
from jax.experimental import pallas as pl
from jax.experimental.pallas import tpu as pltpu

F32 = jnp.float32
BF16 = jnp.bfloat16

D_MODEL = 2048
DEPTH = 4
N_META = 16
D_FF = 5632
ALPHA = (2 * DEPTH) ** 0.25
LN_EPS = 1e-5
RWKV_HEAD_DIM = 64
RWKV_HEADS = D_MODEL // RWKV_HEAD_DIM
RWKV_GN_EPS = 64e-5
RG_BLOCKS = 16
D_RNN = 2688
RG_BLOCK_DIM = D_RNN // RG_BLOCKS
RG_C = 8.0
CONV_W = 4
FOX_HEAD_DIM = 128
FOX_HEADS = D_MODEL // FOX_HEAD_DIM
ML_HEADS = 8
ML_QK = D_MODEL // 2
ML_V = D_MODEL
ML_DQK = ML_QK // ML_HEADS
ML_DV = ML_V // ML_HEADS
ML_EPS = 1e-6

LANE = 128
SUBLANE = 8
ROW_ALIGN = 1280
VMEM_LIMIT = 58 * 1024 * 1024

FFN_TM, FFN_TN = 640, 512
PROJ_TM = 640
OUT_TM = 256
RWKV_PREP_TM, RWKV_PREP_TN = 640, 256
RWKV_CHUNK = 64
RG_TM = 256
RG_KWIN = 512
FOX_TQ = 640
FOX_STRIP = 16
ML_CHUNK = 256

LOG2E = float(np.log2(np.e))


def _cparams(sem):
    return pltpu.CompilerParams(dimension_semantics=sem, vmem_limit_bytes=VMEM_LIMIT)


def _ln(y, g, b):
    mu = jnp.mean(y, -1, keepdims=True)
    yc = y - mu
    var = jnp.mean(yc * yc, -1, keepdims=True)
    return yc * lax.rsqrt(var + LN_EPS) * g + b


def _softplus(z):
    return jnp.maximum(z, 0.0) + jnp.log1p(jnp.exp(-jnp.abs(z)))


def _log_sigmoid(z):
    return -_softplus(-z)


def _scan_add(x, axis):
    n = x.shape[axis]
    idx = lax.broadcasted_iota(jnp.int32, x.shape, axis)
    s = 1
    while s < n:
        x = x + jnp.where(idx >= s, pltpu.roll(x, s, axis), 0.0)
        s *= 2
    return x


def _ffn_ln_body(xb_ref, xp_ref, wg_ref, wu_ref, wo_ref, g_ref, b_ref, o_ref, ob_ref, acc_a, acc_b):
    s = pl.program_id(0)
    nf = D_FF // FFN_TN
    i, j = s // nf, s % nf
    rows_per_step = FFN_TM // (nf - 1)
    n_tiles = (pl.num_programs(0) - (nf - 1)) // nf

    def slice_rows(k):
        return pl.ds(pl.multiple_of(k * rows_per_step, rows_per_step), rows_per_step)

    def ln_slice(acc_prev):
        rows = slice_rows(jnp.minimum(j, nf - 2))
        y = _ln(ALPHA * xp_ref[...] + 0.5 * acc_prev[rows, :], g_ref[...], b_ref[...])
        o_ref[...] = y
        ob_ref[...] = y.astype(BF16)

    def step(acc_cur, acc_prev):
        @pl.when(s == 0)
        def _():
            acc_cur[...] = jnp.zeros_like(acc_cur)
            acc_prev[...] = jnp.zeros_like(acc_prev)

        @pl.when(i < n_tiles)
        def _():
            ln_slice(acc_prev)
            acc_prev[slice_rows(jnp.maximum(j - 1, 0)), :] = jnp.zeros((rows_per_step, D_MODEL), F32)
            xb = xb_ref[...]
            hg = jnp.dot(xb, wg_ref[...], preferred_element_type=F32)
            hu = jnp.dot(xb, wu_ref[...], preferred_element_type=F32)
            act = (hg * jax.nn.sigmoid(hg) * hu).astype(BF16)
            acc_cur[...] += jnp.dot(act, wo_ref[...].astype(BF16), preferred_element_type=F32)

        @pl.when(i >= n_tiles)
        def _():
            ln_slice(acc_prev)

    @pl.when(i % 2 == 0)
    def _():
        step(acc_a, acc_b)

    @pl.when(i % 2 == 1)
    def _():
        step(acc_b, acc_a)


def ffn_ln(h, hb, w_in, w_out, layer, idx, g, b):
    lp = h.shape[0]
    tm, tn = FFN_TM, FFN_TN
    nf = D_FF // tn
    nm = lp // tm
    assert tm % (nf - 1) == 0
    rows = tm // (nf - 1)
    last = nm * nf - 1
    cur = lambda s: (jnp.minimum(s // nf, nm - 1), 0)
    prev = lambda s: (jnp.where(s < nf, 0, (s // nf - 1) * (nf - 1) + jnp.minimum(s % nf, nf - 2)), 0)
    wcol = lambda s: jnp.minimum(s, last) % nf
    return pl.pallas_call(
        _ffn_ln_body,
        grid=(nm * nf + nf - 1,),
        in_specs=[
            pl.BlockSpec((tm, D_MODEL), cur),
            pl.BlockSpec((rows, D_MODEL), prev),
            pl.BlockSpec((None, None, D_MODEL, tn), lambda s: (layer, idx, 0, wcol(s))),
            pl.BlockSpec((None, None, D_MODEL, tn), lambda s: (layer, idx, 0, wcol(s) + nf)),
            pl.BlockSpec((None, None, tn, D_MODEL), lambda s: (layer, idx, wcol(s), 0)),
            pl.BlockSpec((1, D_MODEL), lambda s: (0, 0)),
            pl.BlockSpec((1, D_MODEL), lambda s: (0, 0)),
        ],
        out_specs=[pl.BlockSpec((rows, D_MODEL), prev), pl.BlockSpec((rows, D_MODEL), prev)],
        out_shape=[jax.ShapeDtypeStruct((lp, D_MODEL), F32), jax.ShapeDtypeStruct((lp, D_MODEL), BF16)],
        scratch_shapes=[pltpu.VMEM((tm, D_MODEL), F32), pltpu.VMEM((tm, D_MODEL), F32)],
        compiler_params=_cparams(("arbitrary",)),
        name="ffn_ln",
    )(hb, h, w_in, w_in, w_out, g, b)


def _proj_body(x_ref, w_ref, sc_ref, o_ref, xb_ref):
    @pl.when(pl.program_id(1) == 0)
    def _():
        xb_ref[...] = x_ref[...].astype(BF16)

    acc = jnp.dot(xb_ref[...], w_ref[...], preferred_element_type=F32)
    o_ref[...] = (acc * sc_ref[...]).astype(o_ref.dtype)


def proj(h, w, col_scale, tn, out_dtype):
    lp, n = h.shape[0], w.shape[1]
    tm = PROJ_TM
    return pl.pallas_call(
        _proj_body,
        grid=(lp // tm, n // tn),
        in_specs=[pl.BlockSpec((tm, D_MODEL), lambda i, j: (i, 0)),
                  pl.BlockSpec((D_MODEL, tn), lambda i, j: (0, j)),
                  pl.BlockSpec((1, tn), lambda i, j: (0, j))],
        out_specs=pl.BlockSpec((tm, tn), lambda i, j: (i, j)),
        out_shape=jax.ShapeDtypeStruct((lp, n), out_dtype),
        scratch_shapes=[pltpu.VMEM((tm, D_MODEL), BF16)],
        compiler_params=_cparams(("parallel", "arbitrary")),
        name="proj",
    )(h, w, col_scale)


def _outproj_ln_body(x_ref, y_ref, w_ref, g_ref, b_ref, o_ref, ob_ref):
    acc = jnp.dot(y_ref[...], w_ref[...], preferred_element_type=F32)
    out = _ln(ALPHA * x_ref[...] + acc, g_ref[...], b_ref[...])
    o_ref[...] = out
    ob_ref[...] = out.astype(BF16)


def outproj_ln(h, y, w, g, b):
    lp, k = y.shape
    tm = OUT_TM
    return pl.pallas_call(
        _outproj_ln_body,
        grid=(lp // tm,),
        in_specs=[pl.BlockSpec((tm, D_MODEL), lambda i: (i, 0)),
                  pl.BlockSpec((tm, k), lambda i: (i, 0)),
                  pl.BlockSpec((k, D_MODEL), lambda i: (0, 0)),
                  pl.BlockSpec((1, D_MODEL), lambda i: (0, 0)),
                  pl.BlockSpec((1, D_MODEL), lambda i: (0, 0))],
        out_specs=[pl.BlockSpec((tm, D_MODEL), lambda i: (i, 0))] * 2,
        out_shape=[jax.ShapeDtypeStruct((lp, D_MODEL), F32), jax.ShapeDtypeStruct((lp, D_MODEL), BF16)],
        compiler_params=_cparams(("parallel",)),
        name="outproj_ln",
    )(h, y, w, g, b)


def _rwkv_prep_body(x_ref, xp_ref, mix_ref, wrkv_ref, w1_ref, a1_ref, g1_ref, w2_ref, a2_ref, g2_ref,
                    w0_ref, a0_ref, kk_ref, ka_ref, bd_ref,
                    r_o, lw_o, k_o, v_o, kk_o, b_o, g_o,
                    xx_ref, xs_ref, tw_ref, ta_ref, tg_ref):
    i, j = pl.program_id(0), pl.program_id(1)
    tm = x_ref.shape[0]

    @pl.when(j == 0)
    def _():
        x = x_ref[...]
        prev = jnp.where(i == 0, 0.0, xp_ref[SUBLANE - 1:SUBLANE, :])
        first = lax.broadcasted_iota(jnp.int32, x.shape, 0) == 0
        xx_ref[...] = jnp.where(first, prev, pltpu.roll(x, 1, 0)) - x
        for s in range(3):
            xs_ref[s] = (x + xx_ref[...] * mix_ref[s:s + 1, :]).astype(BF16)
        xs_w = (x + xx_ref[...] * mix_ref[3:4, :]).astype(BF16)
        xs_a = (x + xx_ref[...] * mix_ref[4:5, :]).astype(BF16)
        xs_g = (x + xx_ref[...] * mix_ref[5:6, :]).astype(BF16)
        tw_ref[...] = jnp.tanh(jnp.dot(xs_w, w1_ref[...], preferred_element_type=F32)).astype(BF16)
        ta_ref[...] = jnp.dot(xs_a, a1_ref[...], preferred_element_type=F32).astype(BF16)
        tg_ref[...] = jax.nn.sigmoid(jnp.dot(xs_g, g1_ref[...], preferred_element_type=F32)).astype(BF16)

    r = jnp.dot(xs_ref[0], wrkv_ref[0], preferred_element_type=F32)
    k = jnp.dot(xs_ref[1], wrkv_ref[1], preferred_element_type=F32)
    v = jnp.dot(xs_ref[2], wrkv_ref[2], preferred_element_type=F32)
    wz = w0_ref[...] + jnp.dot(tw_ref[...], w2_ref[...], preferred_element_type=F32)
    az = a0_ref[...] + jnp.dot(ta_ref[...], a2_ref[...], preferred_element_type=F32)
    g = jnp.dot(tg_ref[...], g2_ref[...], preferred_element_type=F32)

    w_log = -_softplus(-wz) - 0.5
    a = jax.nn.sigmoid(az)
    kk = k * kk_ref[...]
    kk2 = kk * kk
    bd = bd_ref[...]
    kk2_hi = kk2.astype(BF16)
    kk2_lo = (kk2 - kk2_hi.astype(F32)).astype(BF16)
    ss = jnp.concatenate(
        [jnp.dot(jnp.concatenate([kk2_hi[:, c * LANE:(c + 1) * LANE], kk2_lo[:, c * LANE:(c + 1) * LANE]], axis=1),
                 bd, preferred_element_type=F32)
         for c in range(kk2.shape[1] // LANE)], axis=1)
    kkn = kk * lax.rsqrt(ss + 1e-12)

    r_o[...] = r
    lw_o[...] = -jnp.exp(w_log)
    k_o[...] = k * (1.0 + (a - 1.0) * ka_ref[...])
    v_o[...] = v
    kk_o[...] = kkn
    b_o[...] = kkn * a
    g_o[...] = g


def rwkv_prep(h, p):
    lp = h.shape[0]
    tm, tn = RWKV_PREP_TM, RWKV_PREP_TN
    row = lambda i, j: (i, 0)
    col = lambda i, j: (0, j)
    full = lambda i, j: (0, 0)
    lw, la, lg = p["w1"].shape[1], p["a1"].shape[1], p["g1"].shape[1]
    out = jax.ShapeDtypeStruct((lp, D_MODEL), F32)
    return pl.pallas_call(
        _rwkv_prep_body,
        grid=(lp // tm, D_MODEL // tn),
        in_specs=[
            pl.BlockSpec((tm, D_MODEL), row),
            pl.BlockSpec((SUBLANE, D_MODEL), lambda i, j: (jnp.maximum(i * (tm // SUBLANE) - 1, 0), 0)),
            pl.BlockSpec((SUBLANE, D_MODEL), full),
            pl.BlockSpec((3, D_MODEL, tn), lambda i, j: (0, 0, j)),
            pl.BlockSpec((D_MODEL, lw), full),
            pl.BlockSpec((D_MODEL, la), full),
            pl.BlockSpec((D_MODEL, lg), full),
            pl.BlockSpec((lw, tn), col),
            pl.BlockSpec((la, tn), col),
            pl.BlockSpec((lg, tn), col),
            pl.BlockSpec((1, tn), col),
            pl.BlockSpec((1, tn), col),
            pl.BlockSpec((1, tn), col),
            pl.BlockSpec((1, tn), col),
            pl.BlockSpec((2 * LANE, LANE), full),
        ],
        out_specs=[pl.BlockSpec((tm, tn), lambda i, j: (i, j))] * 7,
        out_shape=[out] * 7,
        scratch_shapes=[pltpu.VMEM((tm, D_MODEL), F32), pltpu.VMEM((3, tm, D_MODEL), BF16),
                        pltpu.VMEM((tm, lw), BF16), pltpu.VMEM((tm, la), BF16), pltpu.VMEM((tm, lg), BF16)],
        compiler_params=_cparams(("parallel", "arbitrary")),
        name="rwkv_prep",
    )(h, h, p["mix"], p["w_rkv"], p["w1"], p["a1"], p["g1"], p["w2"], p["a2"], p["g2"],
      p["w0"], p["a0"], p["k_k"], p["k_a"], p["bd"])


RWKV_GROUP = 4
RWKV_GW = RWKV_GROUP * RWKV_HEAD_DIM
RWKV_SOLVE_BASE = 8

_NN = ((1,), (0,))
_NT = ((1,), (1,))
_TN = ((0,), (0,))


def _mm(a, b, dims):
    return lax.dot_general(a.astype(BF16), b.astype(BF16), (dims, ((), ())), preferred_element_type=F32)


def _rwkv_chunk_body(r_ref, lw_ref, k_ref, v_ref, kk_ref, b_ref, g_ref, rk_ref, gng_ref, gnb_ref, o_ref, s_ref):
    c = r_ref.shape[0]
    n = RWKV_HEAD_DIM
    gw = RWKV_GW

    @pl.when(pl.program_id(0) == 0)
    def _():
        s_ref[...] = jnp.zeros_like(s_ref)

    lw = lw_ref[...]
    cum = _scan_add(lw, 0)
    e_pos = jnp.exp(cum)
    e_neg = jnp.exp(-cum)
    e_end = e_pos[c - 1:c, :]
    at_all = -kk_ref[...] * jnp.exp(cum - lw)
    rt_all = r_ref[...] * e_pos
    bb_all = b_ref[...] * e_neg
    kb_all = k_ref[...] * e_neg
    rkr_all = r_ref[...] * k_ref[...] * rk_ref[...]

    blk = (lax.broadcasted_iota(jnp.int32, (gw, gw), 0) // n == lax.broadcasted_iota(jnp.int32, (gw, gw), 1) // n)
    blk = blk.astype(F32).astype(BF16)
    row = lax.broadcasted_iota(jnp.int32, (c, gw), 0)
    src = lax.broadcasted_iota(jnp.int32, (c, gw), 1) % n
    strict = src < row
    incl = src <= row

    def bdiag(x):
        return jnp.tile(x.astype(BF16), (RWKV_GROUP, 1)) * blk

    def seg_sums(xs):
        hi = [x.astype(BF16) for x in xs]
        lo = [(x - h_.astype(F32)).astype(BF16) for x, h_ in zip(xs, hi)]
        tot = jnp.dot(jnp.concatenate(hi + lo, axis=0), blk, preferred_element_type=F32)
        rows = xs[0].shape[0]
        return [tot[i * rows:(i + 1) * rows] + tot[(len(xs) + i) * rows:(len(xs) + i + 1) * rows]
                for i in range(len(xs))]

    groups = [slice(g * gw, (g + 1) * gw) for g in range(D_MODEL // gw)]
    v_g = [v_ref[:, sl] for sl in groups]
    lhs = [jnp.concatenate([at_all[:, sl], rt_all[:, sl]], axis=0).astype(BF16) for sl in groups]
    g_b = [_mm(a, bdiag(bb_all[:, sl]), _NT) for a, sl in zip(lhs, groups)]
    g_k = [_mm(a, bdiag(kb_all[:, sl]), _NT) for a, sl in zip(lhs, groups)]
    s0 = [s_ref[g] for g in range(len(groups))]
    z = [_mm(a, bdiag(s_), _NT) for a, s_ in zip(lhs, s0)]
    a_ab = [jnp.where(strict, gm[:c], 0.0) for gm in g_b]
    a_ak = [jnp.where(strict, gm[:c], 0.0) for gm in g_k]
    a_r = [jnp.concatenate([jnp.where(incl, gb[c:], 0.0), jnp.where(incl, gk[c:], 0.0)], axis=1).astype(BF16)
           for gb, gk in zip(g_b, g_k)]
    bd_v = [bdiag(v_) for v_ in v_g]
    rhs = [z_[:c] + _mm(a, bv, _NN) for z_, a, bv in zip(z, a_ak, bd_v)]
    def within(size):
        return (row // size) == (src // size)

    tinv = [jnp.where(within(RWKV_SOLVE_BASE), a, 0.0) for a in a_ab]
    pw = tinv
    tinv = [(src == row).astype(F32) + t_ for t_ in tinv]
    span = 2
    while span < RWKV_SOLVE_BASE:
        pw = [_mm(p_, bdiag(p_), _NN) for p_ in pw]
        tinv = [t_ + _mm(p_, bdiag(t_), _NN) for t_, p_ in zip(tinv, pw)]
        span *= 2
    size = RWKV_SOLVE_BASE
    while size < c:
        link = within(2 * size) & jnp.logical_not(within(size))
        bd_t = [bdiag(t_) for t_ in tinv]
        tl = [_mm(t_, bdiag(jnp.where(link, a, 0.0)), _NN) for t_, a in zip(tinv, a_ab)]
        tinv = [t_ + _mm(x_, b_, _NN) for t_, x_, b_ in zip(tinv, tl, bd_t)]
        size *= 2
    u = [_mm(t_, bdiag(r_), _NN) for t_, r_ in zip(tinv, rhs)]
    y = [z_[c:] + _mm(a, jnp.concatenate([bdiag(u_), bv], axis=0), _NN)
         for z_, a, u_, bv in zip(z, a_r, u, bd_v)]
    for g, sl in enumerate(groups):
        e_end_g = e_end[:, sl]
        uv = jnp.concatenate([u[g], v_g[g]], axis=0)
        bk = jnp.concatenate([bb_all[:, sl], kb_all[:, sl]], axis=0) * e_end_g
        full = _mm(uv, bk, _TN) * blk.astype(F32)
        s_ref[g] = s0[g] * e_end_g + functools.reduce(
            lambda p_, q_: p_ + q_, [full[q * n:(q + 1) * n, :] for q in range(RWKV_GROUP)])
    sums = seg_sums(y + [rkr_all[:, sl] for sl in groups])
    yc = [y_ - m_ * (1.0 / n) for y_, m_ in zip(y, sums[:len(groups)])]
    var = seg_sums([t * t for t in yc])
    for g, sl in enumerate(groups):
        yn = yc[g] * lax.rsqrt(var[g] * (1.0 / n) + RWKV_GN_EPS) * gng_ref[:, sl] + gnb_ref[:, sl]
        bonus = sums[len(groups) + g] * v_g[g]
        o_ref[:, sl] = ((yn + bonus) * g_ref[:, sl]).astype(o_ref.dtype)


def rwkv_chunk(r, lw, k, v, kk, b, g, r_k, gn_g, gn_b):
    lp = r.shape[0]
    c = RWKV_CHUNK
    blk = pl.BlockSpec((c, D_MODEL), lambda i: (i, 0))
    vec = pl.BlockSpec((1, D_MODEL), lambda i: (0, 0))
    return pl.pallas_call(
        _rwkv_chunk_body,
        grid=(lp // c,),
        in_specs=[blk] * 7 + [vec] * 3,
        out_specs=blk,
        out_shape=jax.ShapeDtypeStruct((lp, D_MODEL), BF16),
        scratch_shapes=[pltpu.VMEM((D_MODEL // RWKV_GW, RWKV_HEAD_DIM, RWKV_GW), F32)],
        compiler_params=_cparams(("arbitrary",)),
        name="rwkv_chunk",
    )(r, lw, k, v, kk, b, g, r_k, gn_g, gn_b)


def _pad_rows(a, rows):
    return jnp.pad(a, ((0, rows - a.shape[0]), (0, 0)))


def _pad_cols(a, cols):
    return jnp.pad(a, ((0, 0), (0, cols - a.shape[1])))


def rwkv_params(mix, w_rkv, w0, w1, w2, a0, a1, a2, g1, g2, k_k, k_a, r_k, gn_g, gn_b, w_o):
    lora = LANE
    bd = np.kron(np.eye(LANE // RWKV_HEAD_DIM, dtype=np.float32),
                 np.ones((RWKV_HEAD_DIM, RWKV_HEAD_DIM), np.float32))
    bd = np.concatenate([bd, bd], axis=0)
    row = lambda t: t.reshape(1, D_MODEL)
    return dict(
        mix=_pad_rows(mix, SUBLANE), w_rkv=w_rkv.astype(BF16),
        w1=_pad_cols(w1, lora).astype(BF16), w2=_pad_rows(w2, lora).astype(BF16),
        a1=_pad_cols(a1, lora).astype(BF16), a2=_pad_rows(a2, lora).astype(BF16),
        g1=g1.astype(BF16), g2=g2.astype(BF16),
        w0=row(w0), a0=row(a0), k_k=row(k_k), k_a=row(k_a), bd=jnp.asarray(bd, BF16),
        r_k=row(r_k), gn_g=row(gn_g), gn_b=row(gn_b), w_o=w_o.astype(BF16))


def rwkv_mixer(h, p):
    r, lw, k, v, kk, b, g = rwkv_prep(h, p)
    return rwkv_chunk(r, lw, k, v, kk, b, g, p["r_k"], p["gn_g"], p["gn_b"])


def _rg_windows():
    starts = []
    for c in range(D_RNN // LANE):
        first_block = (c * LANE) // RG_BLOCK_DIM
        ks = (first_block * RG_BLOCK_DIM) // LANE * LANE
        starts.append(min(ks, D_RNN - RG_KWIN))
    return starts


def _rg_compact(w):
    eye = jnp.eye(RG_BLOCKS, dtype=w.dtype)
    dense = (eye[:, None, :, None] * w[:, :, None, :]).reshape(D_RNN, D_RNN)
    tiles = [dense[ks:ks + RG_KWIN, c * LANE:(c + 1) * LANE] for c, ks in enumerate(_rg_windows())]
    return jnp.stack(tiles).astype(BF16)


def _gelu_tanh(x):
    return 0.5 * x * (1.0 + jnp.tanh(np.sqrt(2.0 / np.pi).astype(np.float32) * (x + 0.044715 * (x * x * x))))


def _rg_body(gate_ref, u_ref, up_ref, cw_ref, cb_ref, wa_ref, wx_ref, ba_ref, bx_ref, lam_ref, o_ref,
             ue_ref, carry_ref):
    i = pl.program_id(0)
    tm = u_ref.shape[0]

    @pl.when(i == 0)
    def _():
        carry_ref[...] = jnp.zeros_like(carry_ref)

    u = u_ref[...]
    ue_ref[0:SUBLANE, :] = jnp.where(i == 0, 0.0, up_ref[...])
    ue_ref[SUBLANE:, :] = u
    uc = cb_ref[...] + u * cw_ref[CONV_W - 1:CONV_W, :]
    for d in range(1, CONV_W):
        uc = uc + ue_ref[pl.ds(SUBLANE - d, tm), :] * cw_ref[CONV_W - 1 - d:CONV_W - d, :]
    ucb = uc.astype(BF16)
    za, zx = [], []
    for c, ks in enumerate(_rg_windows()):
        win = ucb[:, ks:ks + RG_KWIN]
        za.append(jnp.dot(win, wa_ref[c], preferred_element_type=F32))
        zx.append(jnp.dot(win, wx_ref[c], preferred_element_type=F32))
    r = jax.nn.sigmoid(jnp.concatenate(za, axis=1) + ba_ref[...])
    ig = jax.nn.sigmoid(jnp.concatenate(zx, axis=1) + bx_ref[...])
    log_a = -RG_C * r * _softplus(-lam_ref[...])
    a = jnp.exp(log_a)
    w2 = -2.0 * jnp.tanh(log_a)
    q = w2 * (1.0 + 0.5 * w2)
    b = jnp.where(q > 0.0, w2 * lax.rsqrt(q), 0.0) * (ig * uc)
    groups = tm // SUBLANE
    a = a.reshape(groups, SUBLANE, D_RNN)
    b = b.reshape(groups, SUBLANE, D_RNN)
    sub = lax.broadcasted_iota(jnp.int32, a.shape, 1)
    s = 1
    while s < SUBLANE:
        keep = sub >= s
        b = jnp.where(keep, a * pltpu.roll(b, s, 1) + b, b)
        a = jnp.where(keep, a * pltpu.roll(a, s, 1), a)
        s *= 2
    carry = carry_ref[0:1, :]
    for g in range(groups):
        hg = a[g] * carry + b[g]
        rows = pl.ds(g * SUBLANE, SUBLANE)
        o_ref[rows, :] = (_gelu_tanh(gate_ref[rows, :]) * hg).astype(o_ref.dtype)
        carry = hg[SUBLANE - 1:SUBLANE, :]
    carry_ref[...] = jnp.broadcast_to(carry, carry_ref.shape)


def rg_params(w_in, conv_w, conv_b, w_a, b_a, w_x, b_x, lam, w_o):
    row = lambda t: t.reshape(1, D_RNN)
    return dict(w_in=w_in.astype(BF16), conv_w=_pad_rows(conv_w, SUBLANE), conv_b=row(conv_b),
                w_a=_rg_compact(w_a), w_x=_rg_compact(w_x), b_a=row(b_a), b_x=row(b_x),
                lam=row(lam), w_o=w_o.astype(BF16), ones=jnp.ones((1, 2 * D_RNN), F32))


def rg_mixer(h, p):
    lp = h.shape[0]
    tm = RG_TM
    gu = proj(h, p["w_in"], p["ones"], 768, F32)
    ntile = D_RNN // LANE
    vec = pl.BlockSpec((1, D_RNN), lambda i: (0, 0))
    wspec = pl.BlockSpec((ntile, RG_KWIN, LANE), lambda i: (0, 0, 0))
    return pl.pallas_call(
        _rg_body,
        grid=(lp // tm,),
        in_specs=[pl.BlockSpec((tm, D_RNN), lambda i: (i, 0)),
                  pl.BlockSpec((tm, D_RNN), lambda i: (i, 1)),
                  pl.BlockSpec((SUBLANE, D_RNN), lambda i: (jnp.maximum(i * (tm // SUBLANE) - 1, 0), 1)),
                  pl.BlockSpec((SUBLANE, D_RNN), lambda i: (0, 0)),
                  vec, wspec, wspec, vec, vec, vec],
        out_specs=pl.BlockSpec((tm, D_RNN), lambda i: (i, 0)),
        out_shape=jax.ShapeDtypeStruct((lp, D_RNN), BF16),
        scratch_shapes=[pltpu.VMEM((tm + SUBLANE, D_RNN), F32), pltpu.VMEM((SUBLANE, D_RNN), F32)],
        compiler_params=_cparams(("arbitrary",)),
        name="rg_lru",
    )(gu, gu, gu, p["conv_w"], p["conv_b"], p["w_a"], p["w_x"], p["b_a"], p["b_x"], p["lam"])


def _split3(x):
    hi = x.astype(BF16).astype(F32)
    mid = (x - hi).astype(BF16).astype(F32)
    lo = (x - hi - mid).astype(BF16).astype(F32)
    return hi, mid, lo


def _fox_gate_body(x_ref, w_ref, bf_ref, cc_ref, qx_ref, kx_ref, carry_ref):
    @pl.when(pl.program_id(0) == 0)
    def _():
        carry_ref[...] = jnp.zeros_like(carry_ref)

    tm = x_ref.shape[0]
    fl = jnp.dot(x_ref[...].astype(BF16), w_ref[...], preferred_element_type=F32)
    c = _scan_add(_log_sigmoid(fl + bf_ref[...]), 0) + carry_ref[0:1, :]
    carry_ref[...] = jnp.broadcast_to(c[tm - 1:, :], carry_ref.shape)
    c = c * LOG2E
    cc_ref[...] = jnp.broadcast_to(c[0:1, :], cc_ref.shape)
    fine = c - c[0:1, :]
    lane = lax.broadcasted_iota(jnp.int32, (tm, LANE), 1)
    ones_mid = jnp.where(lane < 6, 1.0, 0.0)
    for h in range(FOX_HEADS):
        hi, mid, lo = [jnp.broadcast_to(t, (tm, LANE)) for t in _split3(fine[:, h:h + 1])]
        qx = jnp.where(lane == 0, hi, jnp.where(lane == 1, mid, lo))
        kx = -jnp.where(lane == 3, hi, jnp.where(lane == 4, mid, lo))
        qx_ref[:, h * LANE:(h + 1) * LANE] = jnp.where(lane < 3, qx, ones_mid).astype(BF16)
        kx_ref[:, h * LANE:(h + 1) * LANE] = jnp.where(lane < 3, 1.0, jnp.where(lane < 6, kx, 0.0)).astype(BF16)


def fox_gate(h, w_f, b_f):
    lp = h.shape[0]
    tm = FOX_TQ
    nt = lp // tm
    wide = jax.ShapeDtypeStruct((lp, FOX_HEADS * LANE), BF16)
    return pl.pallas_call(
        _fox_gate_body,
        grid=(nt,),
        in_specs=[pl.BlockSpec((tm, D_MODEL), lambda i: (i, 0)),
                  pl.BlockSpec((D_MODEL, LANE), lambda i: (0, 0)),
                  pl.BlockSpec((1, LANE), lambda i: (0, 0))],
        out_specs=[pl.BlockSpec((SUBLANE, LANE), lambda i: (i, 0)),
                   pl.BlockSpec((tm, FOX_HEADS * LANE), lambda i: (i, 0)),
                   pl.BlockSpec((tm, FOX_HEADS * LANE), lambda i: (i, 0))],
        out_shape=[jax.ShapeDtypeStruct((nt * SUBLANE, LANE), F32), wide, wide],
        scratch_shapes=[pltpu.VMEM((SUBLANE, LANE), F32)],
        compiler_params=_cparams(("arbitrary",)),
        name="fox_gate",
    )(h, w_f, b_f)


def _fox_attn_body(cc_ref, q_ref, qx_ref, k_ref, kx_ref, v_ref, o_ref,
                   s_ref, p_ref, mx_ref, d_ref, m_ref, corr_ref, acc_ref):
    h, i = pl.program_id(0), pl.program_id(1)
    tq = q_ref.shape[0]
    nl = tq // LANE
    q_aug = jnp.concatenate([q_ref[...], qx_ref[...]], axis=1)
    ones_col = (lax.broadcasted_iota(jnp.int32, (tq, LANE), 1) == 0).astype(BF16)
    strips = list(range(0, tq, FOX_STRIP))

    def tile_rows(j):
        return pl.ds(pl.multiple_of(j * tq, tq), tq)

    half = tq // 2

    def logits(j, slot, split=False):
        rows = tile_rows(j)
        k_aug = jnp.concatenate([k_ref[rows, :], kx_ref[rows, :]], axis=1)
        if split:
            for r0 in (0, half):
                s_ref[slot, r0:r0 + half, :] = lax.dot_general(q_aug[r0:r0 + half], k_aug, (_NT, ((), ())),
                                                               preferred_element_type=F32)
        else:
            s_ref[slot] = lax.dot_general(q_aug, k_aug, (_NT, ((), ())), preferred_element_type=F32)

    def softmax(j, slot, diagonal):
        delta = cc_ref[h, i] - cc_ref[h, j]
        for r0 in strips:
            rows = pl.ds(r0, FOX_STRIP)
            t = s_ref[slot, rows, :]
            if diagonal:
                row = r0 + lax.broadcasted_iota(jnp.int32, t.shape, 0)
                col = lax.broadcasted_iota(jnp.int32, t.shape, 1)
                t = jnp.where(col <= row, t, -jnp.inf)
                s_ref[slot, rows, :] = t
            mx_ref[rows, :] = functools.reduce(jnp.maximum, [t[:, c * LANE:(c + 1) * LANE] for c in range(nl)])
        m_old = m_ref[...]
        m_new = jnp.maximum(m_old, jnp.max(mx_ref[...], -1, keepdims=True) + delta)
        m_ref[...] = m_new
        corr_ref[slot] = jnp.exp2(m_old - m_new)
        d_ref[...] = jnp.broadcast_to(delta - m_new, d_ref.shape)
        for r0 in strips:
            rows = pl.ds(r0, FOX_STRIP)
            e = s_ref[slot, rows, :] + jnp.tile(d_ref[rows, :], (1, nl))
            p_ref[slot, rows, :] = jnp.exp2(e).astype(BF16)

    def accumulate(j, slot, split=False):
        vb = jnp.concatenate([v_ref[tile_rows(j), :], ones_col], axis=1)
        if split:
            for r0 in (0, half):
                rows = pl.ds(r0, half)
                acc_ref[rows, :] = corr_ref[slot, rows, :] * acc_ref[rows, :] + jnp.dot(
                    p_ref[slot, rows, :], vb, preferred_element_type=F32)
        else:
            acc_ref[...] = corr_ref[slot] * acc_ref[...] + jnp.dot(p_ref[slot], vb, preferred_element_type=F32)

    m_ref[...] = jnp.full(m_ref.shape, -jnp.inf, F32)
    acc_ref[...] = jnp.zeros_like(acc_ref)
    p_ref[1] = jnp.zeros(p_ref.shape[1:], BF16)
    corr_ref[1] = jnp.ones(corr_ref.shape[1:], F32)
    logits(0, 0, split=True)

    def pair(a, carry):
        j = 2 * a
        accumulate(jnp.maximum(j - 1, 0), 1)
        softmax(j, 0, False)
        logits(j + 1, 1)
        accumulate(j, 0)
        softmax(j + 1, 1, False)
        logits(j + 2, 0)
        return carry

    lax.fori_loop(0, i // 2, pair, 0)

    @pl.when(i % 2 == 0)
    def _():
        accumulate(jnp.maximum(i - 1, 0), 1)
        softmax(i, 0, True)
        accumulate(i, 0, split=True)

    @pl.when(i % 2 == 1)
    def _():
        accumulate(jnp.maximum(i - 2, 0), 1)
        softmax(i - 1, 0, False)
        logits(i, 1)
        accumulate(i - 1, 0)
        softmax(i, 1, True)
        accumulate(i, 1, split=True)

    acc = acc_ref[...]
    o_ref[...] = (acc[:, :FOX_HEAD_DIM] / acc[:, FOX_HEAD_DIM:FOX_HEAD_DIM + 1]).astype(o_ref.dtype)


def fox_attn(qkv, qx, kx, cc):
    lp = qkv.shape[0]
    tq = FOX_TQ
    hh = FOX_HEADS
    hd = FOX_HEAD_DIM
    return pl.pallas_call(
        _fox_attn_body,
        grid=(hh, lp // tq),
        in_specs=[pl.BlockSpec(memory_space=pltpu.SMEM),
                  pl.BlockSpec((tq, hd), lambda h, i: (i, h)),
                  pl.BlockSpec((tq, LANE), lambda h, i: (i, h)),
                  pl.BlockSpec((lp, hd), lambda h, i: (0, hh + h)),
                  pl.BlockSpec((lp, LANE), lambda h, i: (0, h)),
                  pl.BlockSpec((lp, hd), lambda h, i: (0, 2 * hh + h))],
        out_specs=pl.BlockSpec((tq, hd), lambda h, i: (i, h)),
        out_shape=jax.ShapeDtypeStruct((lp, D_MODEL), BF16),
        scratch_shapes=[pltpu.VMEM((2, tq, tq), F32), pltpu.VMEM((2, tq, tq), BF16), pltpu.VMEM((tq, LANE), F32),
                        pltpu.VMEM((tq, LANE), F32), pltpu.VMEM((tq, 1), F32), pltpu.VMEM((2, tq, 1), F32),
                        pltpu.VMEM((tq, hd + LANE), F32)],
        compiler_params=_cparams(("parallel", "arbitrary")),
        name="fox_attn",
    )(cc, qkv, qx, qkv, kx, qkv)


def fox_params(w_in, b_f, w_o):
    q_scale = jnp.full((1, D_MODEL), LOG2E * FOX_HEAD_DIM ** -0.5, F32)
    return dict(w_qkv=w_in[:, :3 * D_MODEL].astype(BF16),
                qkv_scale=jnp.concatenate([q_scale, jnp.ones((1, 2 * D_MODEL), F32)], axis=1),
                w_f=_pad_cols(w_in[:, 3 * D_MODEL:], LANE).astype(BF16),
                b_f=_pad_cols(b_f.reshape(1, FOX_HEADS), LANE), w_o=w_o.astype(BF16))


def fox_mixer(h, p):
    qkv = proj(h, p["w_qkv"], p["qkv_scale"], 768, BF16)
    cc, qx, kx = fox_gate(h, p["w_f"], p["b_f"])
    cc = cc[::SUBLANE, :FOX_HEADS].T
    return fox_attn(qkv, qx, kx, cc)


def _mlstm_body(x_ref, gc_ref, gr_ref, bc_ref, br_ref, ng_ref, o_ref, c_ref, n_ref, m_ref):
    cm = x_ref.shape[0]
    hh = ML_HEADS

    @pl.when(pl.program_id(0) == 0)
    def _():
        c_ref[...] = jnp.zeros_like(c_ref)
        n_ref[...] = jnp.zeros_like(n_ref)
        m_ref[...] = jnp.zeros_like(m_ref)

    gc = gc_ref[...] + bc_ref[...]
    bcum_c = _scan_add(_log_sigmoid(gc), 0)
    gr = gr_ref[...] + br_ref[:, 0:1]
    bcum_r = _scan_add(_log_sigmoid(gr), 1)
    row = lax.broadcasted_iota(jnp.int32, (cm, cm), 0)
    col = lax.broadcasted_iota(jnp.int32, (cm, cm), 1)
    causal = col <= row

    for h in range(hh):
        b_col = bcum_c[:, hh + h:hh + h + 1]
        ig_col = gc[:, h:h + 1]
        b_row = bcum_r[hh + h:hh + h + 1, :]
        ig_row = gr[h:h + 1, :]
        m_prev = m_ref[h][0:1, 0:1]
        q = x_ref[:, h * ML_DQK:(h + 1) * ML_DQK].astype(BF16)
        k = x_ref[:, ML_QK + h * ML_DQK:ML_QK + (h + 1) * ML_DQK] * (ML_DQK ** -0.5)
        v = x_ref[:, 2 * ML_QK + h * ML_DV:2 * ML_QK + (h + 1) * ML_DV].astype(BF16)
        og = x_ref[:, 2 * ML_QK + ML_V + h * ML_DV:2 * ML_QK + ML_V + (h + 1) * ML_DV]

        dmat = jnp.where(causal, b_col - b_row + ig_row, -jnp.inf)
        inter = b_col + m_prev
        m_t = jnp.maximum(inter, jnp.max(dmat, -1, keepdims=True))
        w_intra = jnp.exp(dmat - m_t)
        w_inter = jnp.exp(inter - m_t)
        s = lax.dot_general(q, k.astype(BF16), (_NT, ((), ())), preferred_element_type=F32) * w_intra
        c_st = c_ref[h]
        n_st = n_ref[h][0:1, :]
        num = (jnp.dot(s.astype(BF16), v, preferred_element_type=F32)
               + w_inter * jnp.dot(q, c_st.astype(BF16), preferred_element_type=F32))
        den = jnp.sum(s, -1, keepdims=True) + w_inter * jnp.sum(q.astype(F32) * n_st, -1, keepdims=True)
        hv = num / jnp.maximum(jnp.abs(den), jnp.exp(-m_t))

        b_end = b_col[cm - 1:cm, :]
        d_end = b_end - b_col + ig_col
        m_new = jnp.maximum(b_end + m_prev, jnp.max(d_end, 0, keepdims=True))
        kw = k * jnp.exp(d_end - m_new)
        carry = jnp.exp(b_end + m_prev - m_new)
        c_ref[h] = carry * c_st + lax.dot_general(kw.astype(BF16), v, (_TN, ((), ())), preferred_element_type=F32)
        n_ref[h] = jnp.broadcast_to(carry * n_st + jnp.sum(kw, 0, keepdims=True), n_ref.shape[1:])
        m_ref[h] = jnp.broadcast_to(m_new, m_ref.shape[1:])

        hn = hv * lax.rsqrt(jnp.mean(hv * hv, -1, keepdims=True) + ML_EPS)
        sl = slice(h * ML_DV, (h + 1) * ML_DV)
        o_ref[:, sl] = (hn * ng_ref[:, sl] * jax.nn.sigmoid(og)).astype(o_ref.dtype)


def mlstm_params(w_in, b_if, norm_g, w_o):
    nmain = 2 * ML_QK + 2 * ML_V
    b_c = _pad_cols(b_if.reshape(1, 2 * ML_HEADS), LANE)
    b_r = jnp.broadcast_to(b_if.reshape(2 * ML_HEADS, 1), (2 * ML_HEADS, LANE))
    return dict(w_main=w_in[:, :nmain].astype(BF16), w_g=_pad_cols(w_in[:, nmain:], LANE).astype(BF16),
                ones=jnp.ones((1, nmain), F32), b_c=b_c, b_r=b_r, norm_g=norm_g.reshape(1, ML_V),
                w_o=w_o.astype(BF16))


def mlstm_mixer(h, p):
    lp = h.shape[0]
    cm = ML_CHUNK
    nmain = 2 * ML_QK + 2 * ML_V
    main = proj(h, p["w_main"], p["ones"], 768, F32)
    gates = proj(h, p["w_g"], p["ones"][:, :LANE], LANE, F32)
    gates_r = gates[:, :2 * ML_HEADS].T
    return pl.pallas_call(
        _mlstm_body,
        grid=(lp // cm,),
        in_specs=[pl.BlockSpec((cm, nmain), lambda i: (i, 0)),
                  pl.BlockSpec((cm, LANE), lambda i: (i, 0)),
                  pl.BlockSpec((2 * ML_HEADS, cm), lambda i: (0, i)),
                  pl.BlockSpec((1, LANE), lambda i: (0, 0)),
                  pl.BlockSpec((2 * ML_HEADS, LANE), lambda i: (0, 0)),
                  pl.BlockSpec((1, ML_V), lambda i: (0, 0))],
        out_specs=pl.BlockSpec((cm, ML_V), lambda i: (i, 0)),
        out_shape=jax.ShapeDtypeStruct((lp, ML_V), BF16),
        scratch_shapes=[pltpu.VMEM((ML_HEADS, ML_DQK, ML_DV), F32),
                        pltpu.VMEM((ML_HEADS, SUBLANE, ML_DQK), F32),
                        pltpu.VMEM((ML_HEADS, SUBLANE, LANE), F32)],
        compiler_params=_cparams(("arbitrary",)),
        name="mlstm",
    )(main, gates, gates_r, p["b_c"], p["b_r"], p["norm_g"])


def kernel(x, meta_tokens, ln_g, ln_b, ffn_in, ffn_out, rwkv_mix, rwkv_w_rkv, rwkv_w0, rwkv_w1, rwkv_w2, rwkv_a0, rwkv_a1, rwkv_a2, rwkv_g1, rwkv_g2, rwkv_k_k, rwkv_k_a, rwkv_r_k, rwkv_gn_g, rwkv_gn_b, rwkv_w_o, rg_w_in, rg_conv_w, rg_conv_b, rg_w_a, rg_b_a, rg_w_x, rg_b_x, rg_lambda, rg_w_o, fox_w_in, fox_b_f, fox_w_o, ml_w_in, ml_b_if, ml_norm_g, ml_w_o):
    batch, seq, _ = x.shape
    assert batch == 1
    l = seq + N_META
    lp = -(-l // ROW_ALIGN) * ROW_ALIGN
    h = jnp.concatenate([meta_tokens.astype(x.dtype), x[0], jnp.zeros((lp - l, D_MODEL), x.dtype)], axis=0)

    ffn_in_b = ffn_in.astype(BF16)
    rw = rwkv_params(rwkv_mix, rwkv_w_rkv, rwkv_w0, rwkv_w1, rwkv_w2, rwkv_a0, rwkv_a1, rwkv_a2, rwkv_g1,
                     rwkv_g2, rwkv_k_k, rwkv_k_a, rwkv_r_k, rwkv_gn_g, rwkv_gn_b, rwkv_w_o)
    rg = rg_params(rg_w_in, rg_conv_w, rg_conv_b, rg_w_a, rg_b_a, rg_w_x, rg_b_x, rg_lambda, rg_w_o)
    fx = fox_params(fox_w_in, fox_b_f, fox_w_o)
    ml = mlstm_params(ml_w_in, ml_b_if, ml_norm_g, ml_w_o)
    mixers = ((rwkv_mixer, rw), (rg_mixer, rg), (fox_mixer, fx), (mlstm_mixer, ml))

    vec = lambda t: t.reshape(1, D_MODEL)
    hb = h.astype(BF16)
    for layer in range(DEPTH):
        h, hb = ffn_ln(h, hb, ffn_in_b, ffn_out, layer, 0, vec(ln_g[layer, 0]), vec(ln_b[layer, 0]))
        mixer, p = mixers[layer % len(mixers)]
        y = mixer(h, p)
        h, hb = outproj_ln(h, y, p["w_o"], vec(ln_g[layer, 1]), vec(ln_b[layer, 1]))
        h, hb = ffn_ln(h, hb, ffn_in_b, ffn_out, layer, 1, vec(ln_g[layer, 2]), vec(ln_b[layer, 2]))
    return h[N_META:l][None]
```

```python
import functools

import numpy as np
import jax
import jax.numpy as jnp
from jax import lax
from jax.experimental import pallas as pl
from jax.experimental.pallas import tpu as pltpu

F32 = jnp.float32
BF16 = jnp.bfloat16

D_MODEL = 2048
DEPTH = 4
N_META = 16
D_FF = 5632
ALPHA = (2 * DEPTH) ** 0.25
LN_EPS = 1e-5
RWKV_HEAD_DIM = 64
RWKV_HEADS = D_MODEL // RWKV_HEAD_DIM
RWKV_GN_EPS = 64e-5
RG_BLOCKS = 16
D_RNN = 2688
RG_BLOCK_DIM = D_RNN // RG_BLOCKS
RG_C = 8.0
CONV_W = 4
FOX_HEAD_DIM = 128
FOX_HEADS = D_MODEL // FOX_HEAD_DIM
ML_HEADS = 8
ML_QK = D_MODEL // 2
ML_V = D_MODEL
ML_DQK = ML_QK // ML_HEADS
ML_DV = ML_V // ML_HEADS
ML_EPS = 1e-6

LANE = 128
SUBLANE = 8
ROW_ALIGN = 1280
VMEM_LIMIT = 58 * 1024 * 1024

FFN_TM, FFN_TN = 640, 512
PROJ_TM = 640
OUT_TM = 256
RWKV_PREP_TM, RWKV_PREP_TN = 640, 256
RWKV_CHUNK = 64
RG_TM = 256
RG_KWIN = 512
FOX_TQ = 640
FOX_STRIP = 16
ML_CHUNK = 256

LOG2E = float(np.log2(np.e))


def _cparams(sem):
    return pltpu.CompilerParams(dimension_semantics=sem, vmem_limit_bytes=VMEM_LIMIT)


def _ln(y, g, b):
    mu = jnp.mean(y, -1, keepdims=True)
    yc = y - mu
    var = jnp.mean(yc * yc, -1, keepdims=True)
    return yc * lax.rsqrt(var + LN_EPS) * g + b


def _softplus(z):
    return jnp.maximum(z, 0.0) + jnp.log1p(jnp.exp(-jnp.abs(z)))


def _log_sigmoid(z):
    return -_softplus(-z)


def _scan_add(x, axis):
    n = x.shape[axis]
    idx = lax.broadcasted_iota(jnp.int32, x.shape, axis)
    s = 1
    while s < n:
        x = x + jnp.where(idx >= s, pltpu.roll(x, s, axis), 0.0)
        s *= 2
    return x


def _ffn_ln_body(xb_ref, xp_ref, wg_ref, wu_ref, wo_ref, g_ref, b_ref, o_ref, ob_ref, acc_a, acc_b):
    s = pl.program_id(0)
    nf = D_FF // FFN_TN
    i, j = s // nf, s % nf
    rows_per_step = FFN_TM // (nf - 1)
    n_tiles = (pl.num_programs(0) - (nf - 1)) // nf

    def slice_rows(k):
        return pl.ds(pl.multiple_of(k * rows_per_step, rows_per_step), rows_per_step)

    def ln_slice(acc_prev):
        rows = slice_rows(jnp.minimum(j, nf - 2))
        y = _ln(ALPHA * xp_ref[...] + 0.5 * acc_prev[rows, :], g_ref[...], b_ref[...])
        o_ref[...] = y
        ob_ref[...] = y.astype(BF16)

    def step(acc_cur, acc_prev):
        @pl.when(s == 0)
        def _():
            acc_cur[...] = jnp.zeros_like(acc_cur)
            acc_prev[...] = jnp.zeros_like(acc_prev)

        @pl.when(i < n_tiles)
        def _():
            ln_slice(acc_prev)
            acc_prev[slice_rows(jnp.maximum(j - 1, 0)), :] = jnp.zeros((rows_per_step, D_MODEL), F32)
            xb = xb_ref[...]
            hg = jnp.dot(xb, wg_ref[...].astype(BF16), preferred_element_type=F32)
            hu = jnp.dot(xb, wu_ref[...].astype(BF16), preferred_element_type=F32)
            act = (hg * jax.nn.sigmoid(hg) * hu).astype(BF16)
            acc_cur[...] += jnp.dot(act, wo_ref[...].astype(BF16), preferred_element_type=F32)

        @pl.when(i >= n_tiles)
        def _():
            ln_slice(acc_prev)

    @pl.when(i % 2 == 0)
    def _():
        step(acc_a, acc_b)

    @pl.when(i % 2 == 1)
    def _():
        step(acc_b, acc_a)


def ffn_ln(h, hb, w_in, w_out, layer, idx, g, b):
    lp = h.shape[0]
    tm, tn = FFN_TM, FFN_TN
    nf = D_FF // tn
    nm = lp // tm
    assert tm % (nf - 1) == 0
    rows = tm // (nf - 1)
    last = nm * nf - 1
    cur = lambda s: (jnp.minimum(s // nf, nm - 1), 0)
    prev = lambda s: (jnp.where(s < nf, 0, (s // nf - 1) * (nf - 1) + jnp.minimum(s % nf, nf - 2)), 0)
    wcol = lambda s: jnp.minimum(s, last) % nf
    return pl.pallas_call(
        _ffn_ln_body,
        grid=(nm * nf + nf - 1,),
        in_specs=[
            pl.BlockSpec((tm, D_MODEL), cur),
            pl.BlockSpec((rows, D_MODEL), prev),
            pl.BlockSpec((None, None, D_MODEL, tn), lambda s: (layer, idx, 0, wcol(s))),
            pl.BlockSpec((None, None, D_MODEL, tn), lambda s: (layer, idx, 0, wcol(s) + nf)),
            pl.BlockSpec((None, None, tn, D_MODEL), lambda s: (layer, idx, wcol(s), 0)),
            pl.BlockSpec((1, D_MODEL), lambda s: (0, 0)),
            pl.BlockSpec((1, D_MODEL), lambda s: (0, 0)),
        ],
        out_specs=[pl.BlockSpec((rows, D_MODEL), prev), pl.BlockSpec((rows, D_MODEL), prev)],
        out_shape=[jax.ShapeDtypeStruct((lp, D_MODEL), F32), jax.ShapeDtypeStruct((lp, D_MODEL), BF16)],
        scratch_shapes=[pltpu.VMEM((tm, D_MODEL), F32), pltpu.VMEM((tm, D_MODEL), F32)],
        compiler_params=_cparams(("arbitrary",)),
        name="ffn_ln",
    )(hb, h, w_in, w_in, w_out, g, b)


def _proj_body(x_ref, w_ref, sc_ref, o_ref, xb_ref):
    @pl.when(pl.program_id(1) == 0)
    def _():
        xb_ref[...] = x_ref[...].astype(BF16)

    acc = jnp.dot(xb_ref[...], w_ref[...], preferred_element_type=F32)
    o_ref[...] = (acc * sc_ref[...]).astype(o_ref.dtype)


def proj(h, w, col_scale, tn, out_dtype):
    lp, n = h.shape[0], w.shape[1]
    tm = PROJ_TM
    return pl.pallas_call(
        _proj_body,
        grid=(lp // tm, n // tn),
        in_specs=[pl.BlockSpec((tm, D_MODEL), lambda i, j: (i, 0)),
                  pl.BlockSpec((D_MODEL, tn), lambda i, j: (0, j)),
                  pl.BlockSpec((1, tn), lambda i, j: (0, j))],
        out_specs=pl.BlockSpec((tm, tn), lambda i, j: (i, j)),
        out_shape=jax.ShapeDtypeStruct((lp, n), out_dtype),
        scratch_shapes=[pltpu.VMEM((tm, D_MODEL), BF16)],
        compiler_params=_cparams(("parallel", "arbitrary")),
        name="proj",
    )(h, w, col_scale)


def _outproj_ln_body(x_ref, y_ref, w_ref, g_ref, b_ref, o_ref, ob_ref):
    acc = jnp.dot(y_ref[...], w_ref[...], preferred_element_type=F32)
    out = _ln(ALPHA * x_ref[...] + acc, g_ref[...], b_ref[...])
    o_ref[...] = out
    ob_ref[...] = out.astype(BF16)


def outproj_ln(h, y, w, g, b):
    lp, k = y.shape
    tm = OUT_TM
    return pl.pallas_call(
        _outproj_ln_body,
        grid=(lp // tm,),
        in_specs=[pl.BlockSpec((tm, D_MODEL), lambda i: (i, 0)),
                  pl.BlockSpec((tm, k), lambda i: (i, 0)),
                  pl.BlockSpec((k, D_MODEL), lambda i: (0, 0)),
                  pl.BlockSpec((1, D_MODEL), lambda i: (0, 0)),
                  pl.BlockSpec((1, D_MODEL), lambda i: (0, 0))],
        out_specs=[pl.BlockSpec((tm, D_MODEL), lambda i: (i, 0))] * 2,
        out_shape=[jax.ShapeDtypeStruct((lp, D_MODEL), F32), jax.ShapeDtypeStruct((lp, D_MODEL), BF16)],
        compiler_params=_cparams(("parallel",)),
        name="outproj_ln",
    )(h, y, w, g, b)


def _rwkv_prep_body(x_ref, xp_ref, mix_ref, wrkv_ref, w1_ref, a1_ref, g1_ref, w2_ref, a2_ref, g2_ref,
                    w0_ref, a0_ref, kk_ref, ka_ref, bd_ref,
                    r_o, lw_o, k_o, v_o, kk_o, b_o, g_o,
                    xx_ref, xs_ref, tw_ref, ta_ref, tg_ref):
    i, j = pl.program_id(0), pl.program_id(1)
    tm = x_ref.shape[0]

    @pl.when(j == 0)
    def _():
        x = x_ref[...]
        prev = jnp.where(i == 0, 0.0, xp_ref[SUBLANE - 1:SUBLANE, :])
        first = lax.broadcasted_iota(jnp.int32, x.shape, 0) == 0
        xx_ref[...] = jnp.where(first, prev, pltpu.roll(x, 1, 0)) - x
        for s in range(3):
            xs_ref[s] = (x + xx_ref[...] * mix_ref[s:s + 1, :]).astype(BF16)
        xs_w = (x + xx_ref[...] * mix_ref[3:4, :]).astype(BF16)
        xs_a = (x + xx_ref[...] * mix_ref[4:5, :]).astype(BF16)
        xs_g = (x + xx_ref[...] * mix_ref[5:6, :]).astype(BF16)
        tw_ref[...] = jnp.tanh(jnp.dot(xs_w, w1_ref[...], preferred_element_type=F32)).astype(BF16)
        ta_ref[...] = jnp.dot(xs_a, a1_ref[...], preferred_element_type=F32).astype(BF16)
        tg_ref[...] = jax.nn.sigmoid(jnp.dot(xs_g, g1_ref[...], preferred_element_type=F32)).astype(BF16)

    r = jnp.dot(xs_ref[0], wrkv_ref[0], preferred_element_type=F32)
    k = jnp.dot(xs_ref[1], wrkv_ref[1], preferred_element_type=F32)
    v = jnp.dot(xs_ref[2], wrkv_ref[2], preferred_element_type=F32)
    wz = w0_ref[...] + jnp.dot(tw_ref[...], w2_ref[...], preferred_element_type=F32)
    az = a0_ref[...] + jnp.dot(ta_ref[...], a2_ref[...], preferred_element_type=F32)
    g = jnp.dot(tg_ref[...], g2_ref[...], preferred_element_type=F32)

    w_log = -_softplus(-wz) - 0.5
    a = jax.nn.sigmoid(az)
    kk = k * kk_ref[...]
    kk2 = kk * kk
    bd = bd_ref[...]
    kk2_hi = kk2.astype(BF16)
    kk2_lo = (kk2 - kk2_hi.astype(F32)).astype(BF16)
    ss = jnp.concatenate(
        [jnp.dot(jnp.concatenate([kk2_hi[:, c * LANE:(c + 1) * LANE], kk2_lo[:, c * LANE:(c + 1) * LANE]], axis=1),
                 bd, preferred_element_type=F32)
         for c in range(kk2.shape[1] // LANE)], axis=1)
    kkn = kk * lax.rsqrt(ss + 1e-12)

    r_o[...] = r
    lw_o[...] = -jnp.exp(w_log)
    k_o[...] = k * (1.0 + (a - 1.0) * ka_ref[...])
    v_o[...] = v
    kk_o[...] = kkn
    b_o[...] = kkn * a
    g_o[...] = g


def rwkv_prep(h, p):
    lp = h.shape[0]
    tm, tn = RWKV_PREP_TM, RWKV_PREP_TN
    row = lambda i, j: (i, 0)
    col = lambda i, j: (0, j)
    full = lambda i, j: (0, 0)
    lw, la, lg = p["w1"].shape[1], p["a1"].shape[1], p["g1"].shape[1]
    out = jax.ShapeDtypeStruct((lp, D_MODEL), F32)
    return pl.pallas_call(
        _rwkv_prep_body,
        grid=(lp // tm, D_MODEL // tn),
        in_specs=[
            pl.BlockSpec((tm, D_MODEL), row),
            pl.BlockSpec((SUBLANE, D_MODEL), lambda i, j: (jnp.maximum(i * (tm // SUBLANE) - 1, 0), 0)),
            pl.BlockSpec((SUBLANE, D_MODEL), full),
            pl.BlockSpec((3, D_MODEL, tn), lambda i, j: (0, 0, j)),
            pl.BlockSpec((D_MODEL, lw), full),
            pl.BlockSpec((D_MODEL, la), full),
            pl.BlockSpec((D_MODEL, lg), full),
            pl.BlockSpec((lw, tn), col),
            pl.BlockSpec((la, tn), col),
            pl.BlockSpec((lg, tn), col),
            pl.BlockSpec((1, tn), col),
            pl.BlockSpec((1, tn), col),
            pl.BlockSpec((1, tn), col),
            pl.BlockSpec((1, tn), col),
            pl.BlockSpec((2 * LANE, LANE), full),
        ],
        out_specs=[pl.BlockSpec((tm, tn), lambda i, j: (i, j))] * 7,
        out_shape=[out] * 7,
        scratch_shapes=[pltpu.VMEM((tm, D_MODEL), F32), pltpu.VMEM((3, tm, D_MODEL), BF16),
                        pltpu.VMEM((tm, lw), BF16), pltpu.VMEM((tm, la), BF16), pltpu.VMEM((tm, lg), BF16)],
        compiler_params=_cparams(("parallel", "arbitrary")),
        name="rwkv_prep",
    )(h, h, p["mix"], p["w_rkv"], p["w1"], p["a1"], p["g1"], p["w2"], p["a2"], p["g2"],
      p["w0"], p["a0"], p["k_k"], p["k_a"], p["bd"])


RWKV_GROUP = 4
RWKV_GW = RWKV_GROUP * RWKV_HEAD_DIM
RWKV_SOLVE_BASE = 8

_NN = ((1,), (0,))
_NT = ((1,), (1,))
_TN = ((0,), (0,))


def _mm(a, b, dims):
    return lax.dot_general(a.astype(BF16), b.astype(BF16), (dims, ((), ())), preferred_element_type=F32)


def _rwkv_chunk_body(r_ref, lw_ref, k_ref, v_ref, kk_ref, b_ref, g_ref, rk_ref, gng_ref, gnb_ref, o_ref, s_ref):
    c = r_ref.shape[0]
    n = RWKV_HEAD_DIM
    gw = RWKV_GW

    @pl.when(pl.program_id(0) == 0)
    def _():
        s_ref[...] = jnp.zeros_like(s_ref)

    lw = lw_ref[...]
    cum = _scan_add(lw, 0)
    e_pos = jnp.exp(cum)
    e_neg = jnp.exp(-cum)
    e_end = e_pos[c - 1:c, :]
    at_all = -kk_ref[...] * jnp.exp(cum - lw)
    rt_all = r_ref[...] * e_pos
    bb_all = b_ref[...] * e_neg
    kb_all = k_ref[...] * e_neg
    rkr_all = r_ref[...] * k_ref[...] * rk_ref[...]

    blk = (lax.broadcasted_iota(jnp.int32, (gw, gw), 0) // n == lax.broadcasted_iota(jnp.int32, (gw, gw), 1) // n)
    blk = blk.astype(F32).astype(BF16)
    row = lax.broadcasted_iota(jnp.int32, (c, gw), 0)
    src = lax.broadcasted_iota(jnp.int32, (c, gw), 1) % n
    strict = src < row
    incl = src <= row

    def bdiag(x):
        return jnp.tile(x.astype(BF16), (RWKV_GROUP, 1)) * blk

    def seg_sums(xs):
        hi = [x.astype(BF16) for x in xs]
        lo = [(x - h_.astype(F32)).astype(BF16) for x, h_ in zip(xs, hi)]
        tot = jnp.dot(jnp.concatenate(hi + lo, axis=0), blk, preferred_element_type=F32)
        rows = xs[0].shape[0]
        return [tot[i * rows:(i + 1) * rows] + tot[(len(xs) + i) * rows:(len(xs) + i + 1) * rows]
                for i in range(len(xs))]

    groups = [slice(g * gw, (g + 1) * gw) for g in range(D_MODEL // gw)]
    v_g = [v_ref[:, sl] for sl in groups]
    lhs = [jnp.concatenate([at_all[:, sl], rt_all[:, sl]], axis=0).astype(BF16) for sl in groups]
    g_b = [_mm(a, bdiag(bb_all[:, sl]), _NT) for a, sl in zip(lhs, groups)]
    g_k = [_mm(a, bdiag(kb_all[:, sl]), _NT) for a, sl in zip(lhs, groups)]
    s0 = [s_ref[g] for g in range(len(groups))]
    z = [_mm(a, bdiag(s_), _NT) for a, s_ in zip(lhs, s0)]
    a_ab = [jnp.where(strict, gm[:c], 0.0) for gm in g_b]
    a_ak = [jnp.where(strict, gm[:c], 0.0) for gm in g_k]
    a_r = [jnp.concatenate([jnp.where(incl, gb[c:], 0.0), jnp.where(incl, gk[c:], 0.0)], axis=1).astype(BF16)
           for gb, gk in zip(g_b, g_k)]
    bd_v = [bdiag(v_) for v_ in v_g]
    rhs = [z_[:c] + _mm(a, bv, _NN) for z_, a, bv in zip(z, a_ak, bd_v)]
    def within(size):
        return (row // size) == (src // size)

    tinv = [jnp.where(within(RWKV_SOLVE_BASE), a, 0.0) for a in a_ab]
    pw = tinv
    tinv = [(src == row).astype(F32) + t_ for t_ in tinv]
    span = 2
    while span < RWKV_SOLVE_BASE:
        pw = [_mm(p_, bdiag(p_), _NN) for p_ in pw]
        tinv = [t_ + _mm(p_, bdiag(t_), _NN) for t_, p_ in zip(tinv, pw)]
        span *= 2
    size = RWKV_SOLVE_BASE
    while size < c:
        link = within(2 * size) & jnp.logical_not(within(size))
        bd_t = [bdiag(t_) for t_ in tinv]
        tl = [_mm(t_, bdiag(jnp.where(link, a, 0.0)), _NN) for t_, a in zip(tinv, a_ab)]
        tinv = [t_ + _mm(x_, b_, _NN) for t_, x_, b_ in zip(tinv, tl, bd_t)]
        size *= 2
    u = [_mm(t_, bdiag(r_), _NN) for t_, r_ in zip(tinv, rhs)]
    y = [z_[c:] + _mm(a, jnp.concatenate([bdiag(u_), bv], axis=0), _NN)
         for z_, a, u_, bv in zip(z, a_r, u, bd_v)]
    for g, sl in enumerate(groups):
        e_end_g = e_end[:, sl]
        uv = jnp.concatenate([u[g], v_g[g]], axis=0)
        bk = jnp.concatenate([bb_all[:, sl], kb_all[:, sl]], axis=0) * e_end_g
        full = _mm(uv, bk, _TN) * blk.astype(F32)
        s_ref[g] = s0[g] * e_end_g + functools.reduce(
            lambda p_, q_: p_ + q_, [full[q * n:(q + 1) * n, :] for q in range(RWKV_GROUP)])
    sums = seg_sums(y + [rkr_all[:, sl] for sl in groups])
    yc = [y_ - m_ * (1.0 / n) for y_, m_ in zip(y, sums[:len(groups)])]
    var = seg_sums([t * t for t in yc])
    for g, sl in enumerate(groups):
        yn = yc[g] * lax.rsqrt(var[g] * (1.0 / n) + RWKV_GN_EPS) * gng_ref[:, sl] + gnb_ref[:, sl]
        bonus = sums[len(groups) + g] * v_g[g]
        o_ref[:, sl] = ((yn + bonus) * g_ref[:, sl]).astype(o_ref.dtype)


def rwkv_chunk(r, lw, k, v, kk, b, g, r_k, gn_g, gn_b):
    lp = r.shape[0]
    c = RWKV_CHUNK
    blk = pl.BlockSpec((c, D_MODEL), lambda i: (i, 0))
    vec = pl.BlockSpec((1, D_MODEL), lambda i: (0, 0))
    return pl.pallas_call(
        _rwkv_chunk_body,
        grid=(lp // c,),
        in_specs=[blk] * 7 + [vec] * 3,
        out_specs=blk,
        out_shape=jax.ShapeDtypeStruct((lp, D_MODEL), BF16),
        scratch_shapes=[pltpu.VMEM((D_MODEL // RWKV_GW, RWKV_HEAD_DIM, RWKV_GW), F32)],
        compiler_params=_cparams(("arbitrary",)),
        name="rwkv_chunk",
    )(r, lw, k, v, kk, b, g, r_k, gn_g, gn_b)


def _pad_rows(a, rows):
    return jnp.pad(a, ((0, rows - a.shape[0]), (0, 0)))


def _pad_cols(a, cols):
    return jnp.pad(a, ((0, 0), (0, cols - a.shape[1])))


def rwkv_params(mix, w_rkv, w0, w1, w2, a0, a1, a2, g1, g2, k_k, k_a, r_k, gn_g, gn_b, w_o):
    lora = LANE
    bd = np.kron(np.eye(LANE // RWKV_HEAD_DIM, dtype=np.float32),
                 np.ones((RWKV_HEAD_DIM, RWKV_HEAD_DIM), np.float32))
    bd = np.concatenate([bd, bd], axis=0)
    row = lambda t: t.reshape(1, D_MODEL)
    return dict(
        mix=_pad_rows(mix, SUBLANE), w_rkv=w_rkv.astype(BF16),
        w1=_pad_cols(w1, lora).astype(BF16), w2=_pad_rows(w2, lora).astype(BF16),
        a1=_pad_cols(a1, lora).astype(BF16), a2=_pad_rows(a2, lora).astype(BF16),
        g1=g1.astype(BF16), g2=g2.astype(BF16),
        w0=row(w0), a0=row(a0), k_k=row(k_k), k_a=row(k_a), bd=jnp.asarray(bd, BF16),
        r_k=row(r_k), gn_g=row(gn_g), gn_b=row(gn_b), w_o=w_o.astype(BF16))


def rwkv_mixer(h, p):
    r, lw, k, v, kk, b, g = rwkv_prep(h, p)
    return rwkv_chunk(r, lw, k, v, kk, b, g, p["r_k"], p["gn_g"], p["gn_b"])


def _rg_windows():
    starts = []
    for c in range(D_RNN // LANE):
        first_block = (c * LANE) // RG_BLOCK_DIM
        ks = (first_block * RG_BLOCK_DIM) // LANE * LANE
        starts.append(min(ks, D_RNN - RG_KWIN))
    return starts


def _rg_compact(w):
    eye = jnp.eye(RG_BLOCKS, dtype=w.dtype)
    dense = (eye[:, None, :, None] * w[:, :, None, :]).reshape(D_RNN, D_RNN)
    tiles = [dense[ks:ks + RG_KWIN, c * LANE:(c + 1) * LANE] for c, ks in enumerate(_rg_windows())]
    return jnp.stack(tiles).astype(BF16)


def _gelu_tanh(x):
    return 0.5 * x * (1.0 + jnp.tanh(np.sqrt(2.0 / np.pi).astype(np.float32) * (x + 0.044715 * (x * x * x))))


def _rg_body(gate_ref, u_ref, up_ref, cw_ref, cb_ref, wa_ref, wx_ref, ba_ref, bx_ref, lam_ref, o_ref,
             ue_ref, carry_ref):
    i = pl.program_id(0)
    tm = u_ref.shape[0]

    @pl.when(i == 0)
    def _():
        carry_ref[...] = jnp.zeros_like(carry_ref)

    u = u_ref[...]
    ue_ref[0:SUBLANE, :] = jnp.where(i == 0, 0.0, up_ref[...])
    ue_ref[SUBLANE:, :] = u
    uc = cb_ref[...] + u * cw_ref[CONV_W - 1:CONV_W, :]
    for d in range(1, CONV_W):
        uc = uc + ue_ref[pl.ds(SUBLANE - d, tm), :] * cw_ref[CONV_W - 1 - d:CONV_W - d, :]
    ucb = uc.astype(BF16)
    za, zx = [], []
    for c, ks in enumerate(_rg_windows()):
        win = ucb[:, ks:ks + RG_KWIN]
        za.append(jnp.dot(win, wa_ref[c], preferred_element_type=F32))
        zx.append(jnp.dot(win, wx_ref[c], preferred_element_type=F32))
    r = jax.nn.sigmoid(jnp.concatenate(za, axis=1) + ba_ref[...])
    ig = jax.nn.sigmoid(jnp.concatenate(zx, axis=1) + bx_ref[...])
    log_a = -RG_C * r * _softplus(-lam_ref[...])
    a = jnp.exp(log_a)
    w2 = -2.0 * jnp.tanh(log_a)
    q = w2 * (1.0 + 0.5 * w2)
    b = jnp.where(q > 0.0, w2 * lax.rsqrt(q), 0.0) * (ig * uc)
    groups = tm // SUBLANE
    a = a.reshape(groups, SUBLANE, D_RNN)
    b = b.reshape(groups, SUBLANE, D_RNN)
    sub = lax.broadcasted_iota(jnp.int32, a.shape, 1)
    s = 1
    while s < SUBLANE:
        keep = sub >= s
        b = jnp.where(keep, a * pltpu.roll(b, s, 1) + b, b)
        a = jnp.where(keep, a * pltpu.roll(a, s, 1), a)
        s *= 2
    carry = carry_ref[0:1, :]
    for g in range(groups):
        hg = a[g] * carry + b[g]
        rows = pl.ds(g * SUBLANE, SUBLANE)
        o_ref[rows, :] = (_gelu_tanh(gate_ref[rows, :]) * hg).astype(o_ref.dtype)
        carry = hg[SUBLANE - 1:SUBLANE, :]
    carry_ref[...] = jnp.broadcast_to(carry, carry_ref.shape)


def rg_params(w_in, conv_w, conv_b, w_a, b_a, w_x, b_x, lam, w_o):
    row = lambda t: t.reshape(1, D_RNN)
    return dict(w_in=w_in.astype(BF16), conv_w=_pad_rows(conv_w, SUBLANE), conv_b=row(conv_b),
                w_a=_rg_compact(w_a), w_x=_rg_compact(w_x), b_a=row(b_a), b_x=row(b_x),
                lam=row(lam), w_o=w_o.astype(BF16), ones=jnp.ones((1, 2 * D_RNN), F32))


def rg_mixer(h, p):
    lp = h.shape[0]
    tm = RG_TM
    gu = proj(h, p["w_in"], p["ones"], 768, F32)
    ntile = D_RNN // LANE
    vec = pl.BlockSpec((1, D_RNN), lambda i: (0, 0))
    wspec = pl.BlockSpec((ntile, RG_KWIN, LANE), lambda i: (0, 0, 0))
    return pl.pallas_call(
        _rg_body,
        grid=(lp // tm,),
        in_specs=[pl.BlockSpec((tm, D_RNN), lambda i: (i, 0)),
                  pl.BlockSpec((tm, D_RNN), lambda i: (i, 1)),
                  pl.BlockSpec((SUBLANE, D_RNN), lambda i: (jnp.maximum(i * (tm // SUBLANE) - 1, 0), 1)),
                  pl.BlockSpec((SUBLANE, D_RNN), lambda i: (0, 0)),
                  vec, wspec, wspec, vec, vec, vec],
        out_specs=pl.BlockSpec((tm, D_RNN), lambda i: (i, 0)),
        out_shape=jax.ShapeDtypeStruct((lp, D_RNN), BF16),
        scratch_shapes=[pltpu.VMEM((tm + SUBLANE, D_RNN), F32), pltpu.VMEM((SUBLANE, D_RNN), F32)],
        compiler_params=_cparams(("arbitrary",)),
        name="rg_lru",
    )(gu, gu, gu, p["conv_w"], p["conv_b"], p["w_a"], p["w_x"], p["b_a"], p["b_x"], p["lam"])


def _split3(x):
    hi = x.astype(BF16).astype(F32)
    mid = (x - hi).astype(BF16).astype(F32)
    lo = (x - hi - mid).astype(BF16).astype(F32)
    return hi, mid, lo


def _fox_gate_body(x_ref, w_ref, bf_ref, cc_ref, qx_ref, kx_ref, carry_ref):
    @pl.when(pl.program_id(0) == 0)
    def _():
        carry_ref[...] = jnp.zeros_like(carry_ref)

    tm = x_ref.shape[0]
    fl = jnp.dot(x_ref[...].astype(BF16), w_ref[...], preferred_element_type=F32)
    c = _scan_add(_log_sigmoid(fl + bf_ref[...]), 0) + carry_ref[0:1, :]
    carry_ref[...] = jnp.broadcast_to(c[tm - 1:, :], carry_ref.shape)
    c = c * LOG2E
    cc_ref[...] = jnp.broadcast_to(c[0:1, :], cc_ref.shape)
    fine = c - c[0:1, :]
    lane = lax.broadcasted_iota(jnp.int32, (tm, LANE), 1)
    ones_mid = jnp.where(lane < 6, 1.0, 0.0)
    for h in range(FOX_HEADS):
        hi, mid, lo = [jnp.broadcast_to(t, (tm, LANE)) for t in _split3(fine[:, h:h + 1])]
        qx = jnp.where(lane == 0, hi, jnp.where(lane == 1, mid, lo))
        kx = -jnp.where(lane == 3, hi, jnp.where(lane == 4, mid, lo))
        qx_ref[:, h * LANE:(h + 1) * LANE] = jnp.where(lane < 3, qx, ones_mid).astype(BF16)
        kx_ref[:, h * LANE:(h + 1) * LANE] = jnp.where(lane < 3, 1.0, jnp.where(lane < 6, kx, 0.0)).astype(BF16)


def fox_gate(h, w_f, b_f):
    lp = h.shape[0]
    tm = FOX_TQ
    nt = lp // tm
    wide = jax.ShapeDtypeStruct((lp, FOX_HEADS * LANE), BF16)
    return pl.pallas_call(
        _fox_gate_body,
        grid=(nt,),
        in_specs=[pl.BlockSpec((tm, D_MODEL), lambda i: (i, 0)),
                  pl.BlockSpec((D_MODEL, LANE), lambda i: (0, 0)),
                  pl.BlockSpec((1, LANE), lambda i: (0, 0))],
        out_specs=[pl.BlockSpec((SUBLANE, LANE), lambda i: (i, 0)),
                   pl.BlockSpec((tm, FOX_HEADS * LANE), lambda i: (i, 0)),
                   pl.BlockSpec((tm, FOX_HEADS * LANE), lambda i: (i, 0))],
        out_shape=[jax.ShapeDtypeStruct((nt * SUBLANE, LANE), F32), wide, wide],
        scratch_shapes=[pltpu.VMEM((SUBLANE, LANE), F32)],
        compiler_params=_cparams(("arbitrary",)),
        name="fox_gate",
    )(h, w_f, b_f)


def _fox_attn_body(cc_ref, q_ref, qx_ref, k_ref, kx_ref, v_ref, o_ref,
                   s_ref, p_ref, mx_ref, d_ref, m_ref, corr_ref, acc_ref):
    h, i = pl.program_id(0), pl.program_id(1)
    tq = q_ref.shape[0]
    nl = tq // LANE
    q_aug = jnp.concatenate([q_ref[...], qx_ref[...]], axis=1)
    ones_col = (lax.broadcasted_iota(jnp.int32, (tq, LANE), 1) == 0).astype(BF16)
    strips = list(range(0, tq, FOX_STRIP))

    def tile_rows(j):
        return pl.ds(pl.multiple_of(j * tq, tq), tq)

    half = tq // 2

    def logits(j, slot, split=False):
        rows = tile_rows(j)
        k_aug = jnp.concatenate([k_ref[rows, :], kx_ref[rows, :]], axis=1)
        if split:
            for r0 in (0, half):
                s_ref[slot, r0:r0 + half, :] = lax.dot_general(q_aug[r0:r0 + half], k_aug, (_NT, ((), ())),
                                                               preferred_element_type=F32)
        else:
            s_ref[slot] = lax.dot_general(q_aug, k_aug, (_NT, ((), ())), preferred_element_type=F32)

    def softmax(j, slot, diagonal):
        delta = cc_ref[h, i] - cc_ref[h, j]
        for r0 in strips:
            rows = pl.ds(r0, FOX_STRIP)
            t = s_ref[slot, rows, :]
            if diagonal:
                row = r0 + lax.broadcasted_iota(jnp.int32, t.shape, 0)
                col = lax.broadcasted_iota(jnp.int32, t.shape, 1)
                t = jnp.where(col <= row, t, -jnp.inf)
                s_ref[slot, rows, :] = t
            mx_ref[rows, :] = functools.reduce(jnp.maximum, [t[:, c * LANE:(c + 1) * LANE] for c in range(nl)])
        m_old = m_ref[...]
        m_new = jnp.maximum(m_old, jnp.max(mx_ref[...], -1, keepdims=True) + delta)
        m_ref[...] = m_new
        corr_ref[slot] = jnp.exp2(m_old - m_new)
        d_ref[...] = jnp.broadcast_to(delta - m_new, d_ref.shape)
        for r0 in strips:
            rows = pl.ds(r0, FOX_STRIP)
            e = s_ref[slot, rows, :] + jnp.tile(d_ref[rows, :], (1, nl))
            p_ref[slot, rows, :] = jnp.exp2(e).astype(BF16)

    def accumulate(j, slot, split=False):
        vb = jnp.concatenate([v_ref[tile_rows(j), :], ones_col], axis=1)
        if split:
            for r0 in (0, half):
                rows = pl.ds(r0, half)
                acc_ref[rows, :] = corr_ref[slot, rows, :] * acc_ref[rows, :] + jnp.dot(
                    p_ref[slot, rows, :], vb, preferred_element_type=F32)
        else:
            acc_ref[...] = corr_ref[slot] * acc_ref[...] + jnp.dot(p_ref[slot], vb, preferred_element_type=F32)

    m_ref[...] = jnp.full(m_ref.shape, -jnp.inf, F32)
    acc_ref[...] = jnp.zeros_like(acc_ref)
    p_ref[1] = jnp.zeros(p_ref.shape[1:], BF16)
    corr_ref[1] = jnp.ones(corr_ref.shape[1:], F32)
    logits(0, 0, split=True)

    def pair(a, carry):
        j = 2 * a
        accumulate(jnp.maximum(j - 1, 0), 1)
        softmax(j, 0, False)
        logits(j + 1, 1)
        accumulate(j, 0)
        softmax(j + 1, 1, False)
        logits(j + 2, 0)
        return carry

    lax.fori_loop(0, i // 2, pair, 0)

    @pl.when(i % 2 == 0)
    def _():
        accumulate(jnp.maximum(i - 1, 0), 1)
        softmax(i, 0, True)
        accumulate(i, 0, split=True)

    @pl.when(i % 2 == 1)
    def _():
        accumulate(jnp.maximum(i - 2, 0), 1)
        softmax(i - 1, 0, False)
        logits(i, 1)
        accumulate(i - 1, 0)
        softmax(i, 1, True)
        accumulate(i, 1, split=True)

    acc = acc_ref[...]
    o_ref[...] = (acc[:, :FOX_HEAD_DIM] / acc[:, FOX_HEAD_DIM:FOX_HEAD_DIM + 1]).astype(o_ref.dtype)


def fox_attn(qkv, qx, kx, cc):
    lp = qkv.shape[0]
    tq = FOX_TQ
    hh = FOX_HEADS
    hd = FOX_HEAD_DIM
    return pl.pallas_call(
        _fox_attn_body,
        grid=(hh, lp // tq),
        in_specs=[pl.BlockSpec(memory_space=pltpu.SMEM),
                  pl.BlockSpec((tq, hd), lambda h, i: (i, h)),
                  pl.BlockSpec((tq, LANE), lambda h, i: (i, h)),
                  pl.BlockSpec((lp, hd), lambda h, i: (0, hh + h)),
                  pl.BlockSpec((lp, LANE), lambda h, i: (0, h)),
                  pl.BlockSpec((lp, hd), lambda h, i: (0, 2 * hh + h))],
        out_specs=pl.BlockSpec((tq, hd), lambda h, i: (i, h)),
        out_shape=jax.ShapeDtypeStruct((lp, D_MODEL), BF16),
        scratch_shapes=[pltpu.VMEM((2, tq, tq), F32), pltpu.VMEM((2, tq, tq), BF16), pltpu.VMEM((tq, LANE), F32),
                        pltpu.VMEM((tq, LANE), F32), pltpu.VMEM((tq, 1), F32), pltpu.VMEM((2, tq, 1), F32),
                        pltpu.VMEM((tq, hd + LANE), F32)],
        compiler_params=_cparams(("parallel", "arbitrary")),
        name="fox_attn",
    )(cc, qkv, qx, qkv, kx, qkv)


def fox_params(w_in, b_f, w_o):
    q_scale = jnp.full((1, D_MODEL), LOG2E * FOX_HEAD_DIM ** -0.5, F32)
    return dict(w_qkv=w_in[:, :3 * D_MODEL].astype(BF16),
                qkv_scale=jnp.concatenate([q_scale, jnp.ones((1, 2 * D_MODEL), F32)], axis=1),
                w_f=_pad_cols(w_in[:, 3 * D_MODEL:], LANE).astype(BF16),
                b_f=_pad_cols(b_f.reshape(1, FOX_HEADS), LANE), w_o=w_o.astype(BF16))


def fox_mixer(h, p):
    qkv = proj(h, p["w_qkv"], p["qkv_scale"], 768, BF16)
    cc, qx, kx = fox_gate(h, p["w_f"], p["b_f"])
    cc = cc[::SUBLANE, :FOX_HEADS].T
    return fox_attn(qkv, qx, kx, cc)


def _mlstm_body(x_ref, gc_ref, gr_ref, bc_ref, br_ref, ng_ref, o_ref, c_ref, n_ref, m_ref):
    cm = x_ref.shape[0]
    hh = ML_HEADS

    @pl.when(pl.program_id(0) == 0)
    def _():
        c_ref[...] = jnp.zeros_like(c_ref)
        n_ref[...] = jnp.zeros_like(n_ref)
        m_ref[...] = jnp.zeros_like(m_ref)

    gc = gc_ref[...] + bc_ref[...]
    bcum_c = _scan_add(_log_sigmoid(gc), 0)
    gr = gr_ref[...] + br_ref[:, 0:1]
    bcum_r = _scan_add(_log_sigmoid(gr), 1)
    row = lax.broadcasted_iota(jnp.int32, (cm, cm), 0)
    col = lax.broadcasted_iota(jnp.int32, (cm, cm), 1)
    causal = col <= row

    for h in range(hh):
        b_col = bcum_c[:, hh + h:hh + h + 1]
        ig_col = gc[:, h:h + 1]
        b_row = bcum_r[hh + h:hh + h + 1, :]
        ig_row = gr[h:h + 1, :]
        m_prev = m_ref[h][0:1, 0:1]
        q = x_ref[:, h * ML_DQK:(h + 1) * ML_DQK].astype(BF16)
        k = x_ref[:, ML_QK + h * ML_DQK:ML_QK + (h + 1) * ML_DQK] * (ML_DQK ** -0.5)
        v = x_ref[:, 2 * ML_QK + h * ML_DV:2 * ML_QK + (h + 1) * ML_DV].astype(BF16)
        og = x_ref[:, 2 * ML_QK + ML_V + h * ML_DV:2 * ML_QK + ML_V + (h + 1) * ML_DV]

        dmat = jnp.where(causal, b_col - b_row + ig_row, -jnp.inf)
        inter = b_col + m_prev
        m_t = jnp.maximum(inter, jnp.max(dmat, -1, keepdims=True))
        w_intra = jnp.exp(dmat - m_t)
        w_inter = jnp.exp(inter - m_t)
        s = lax.dot_general(q, k.astype(BF16), (_NT, ((), ())), preferred_element_type=F32) * w_intra
        c_st = c_ref[h]
        n_st = n_ref[h][0:1, :]
        num = (jnp.dot(s.astype(BF16), v, preferred_element_type=F32)
               + w_inter * jnp.dot(q, c_st.astype(BF16), preferred_element_type=F32))
        den = jnp.sum(s, -1, keepdims=True) + w_inter * jnp.sum(q.astype(F32) * n_st, -1, keepdims=True)
        hv = num / jnp.maximum(jnp.abs(den), jnp.exp(-m_t))

        b_end = b_col[cm - 1:cm, :]
        d_end = b_end - b_col + ig_col
        m_new = jnp.maximum(b_end + m_prev, jnp.max(d_end, 0, keepdims=True))
        kw = k * jnp.exp(d_end - m_new)
        carry = jnp.exp(b_end + m_prev - m_new)
        c_ref[h] = carry * c_st + lax.dot_general(kw.astype(BF16), v, (_TN, ((), ())), preferred_element_type=F32)
        n_ref[h] = jnp.broadcast_to(carry * n_st + jnp.sum(kw, 0, keepdims=True), n_ref.shape[1:])
        m_ref[h] = jnp.broadcast_to(m_new, m_ref.shape[1:])

        hn = hv * lax.rsqrt(jnp.mean(hv * hv, -1, keepdims=True) + ML_EPS)
        sl = slice(h * ML_DV, (h + 1) * ML_DV)
        o_ref[:, sl] = (hn * ng_ref[:, sl] * jax.nn.sigmoid(og)).astype(o_ref.dtype)


def mlstm_params(w_in, b_if, norm_g, w_o):
    nmain = 2 * ML_QK + 2 * ML_V
    b_c = _pad_cols(b_if.reshape(1, 2 * ML_HEADS), LANE)
    b_r = jnp.broadcast_to(b_if.reshape(2 * ML_HEADS, 1), (2 * ML_HEADS, LANE))
    return dict(w_main=w_in[:, :nmain].astype(BF16), w_g=_pad_cols(w_in[:, nmain:], LANE).astype(BF16),
                ones=jnp.ones((1, nmain), F32), b_c=b_c, b_r=b_r, norm_g=norm_g.reshape(1, ML_V),
                w_o=w_o.astype(BF16))


def mlstm_mixer(h, p):
    lp = h.shape[0]
    cm = ML_CHUNK
    nmain = 2 * ML_QK + 2 * ML_V
    main = proj(h, p["w_main"], p["ones"], 768, F32)
    gates = proj(h, p["w_g"], p["ones"][:, :LANE], LANE, F32)
    gates_r = gates[:, :2 * ML_HEADS].T
    return pl.pallas_call(
        _mlstm_body,
        grid=(lp // cm,),
        in_specs=[pl.BlockSpec((cm, nmain), lambda i: (i, 0)),
                  pl.BlockSpec((cm, LANE), lambda i: (i, 0)),
                  pl.BlockSpec((2 * ML_HEADS, cm), lambda i: (0, i)),
                  pl.BlockSpec((1, LANE), lambda i: (0, 0)),
                  pl.BlockSpec((2 * ML_HEADS, LANE), lambda i: (0, 0)),
                  pl.BlockSpec((1, ML_V), lambda i: (0, 0))],
        out_specs=pl.BlockSpec((cm, ML_V), lambda i: (i, 0)),
        out_shape=jax.ShapeDtypeStruct((lp, ML_V), BF16),
        scratch_shapes=[pltpu.VMEM((ML_HEADS, ML_DQK, ML_DV), F32),
                        pltpu.VMEM((ML_HEADS, SUBLANE, ML_DQK), F32),
                        pltpu.VMEM((ML_HEADS, SUBLANE, LANE), F32)],
        compiler_params=_cparams(("arbitrary",)),
        name="mlstm",
    )(main, gates, gates_r, p["b_c"], p["b_r"], p["norm_g"])


def kernel(x, meta_tokens, ln_g, ln_b, ffn_in, ffn_out, rwkv_mix, rwkv_w_rkv, rwkv_w0, rwkv_w1, rwkv_w2, rwkv_a0, rwkv_a1, rwkv_a2, rwkv_g1, rwkv_g2, rwkv_k_k, rwkv_k_a, rwkv_r_k, rwkv_gn_g, rwkv_gn_b, rwkv_w_o, rg_w_in, rg_conv_w, rg_conv_b, rg_w_a, rg_b_a, rg_w_x, rg_b_x, rg_lambda, rg_w_o, fox_w_in, fox_b_f, fox_w_o, ml_w_in, ml_b_if, ml_norm_g, ml_w_o):
    batch, seq, _ = x.shape
    assert batch == 1
    l = seq + N_META
    lp = -(-l // ROW_ALIGN) * ROW_ALIGN
    h = jnp.concatenate([meta_tokens.astype(x.dtype), x[0], jnp.zeros((lp - l, D_MODEL), x.dtype)], axis=0)

    rw = rwkv_params(rwkv_mix, rwkv_w_rkv, rwkv_w0, rwkv_w1, rwkv_w2, rwkv_a0, rwkv_a1, rwkv_a2, rwkv_g1,
                     rwkv_g2, rwkv_k_k, rwkv_k_a, rwkv_r_k, rwkv_gn_g, rwkv_gn_b, rwkv_w_o)
    rg = rg_params(rg_w_in, rg_conv_w, rg_conv_b, rg_w_a, rg_b_a, rg_w_x, rg_b_x, rg_lambda, rg_w_o)
    fx = fox_params(fox_w_in, fox_b_f, fox_w_o)
    ml = mlstm_params(ml_w_in, ml_b_if, ml_norm_g, ml_w_o)
    mixers = ((rwkv_mixer, rw), (rg_mixer, rg), (fox_mixer, fx), (mlstm_mixer, ml))

    vec = lambda t: t.reshape(1, D_MODEL)
    hb = h.astype(BF16)
    for layer in range(DEPTH):
        h, hb = ffn_ln(h, hb, ffn_in, ffn_out, layer, 0, vec(ln_g[layer, 0]), vec(ln_b[layer, 0]))
        mixer, p = mixers[layer % len(mixers)]
        y = mixer(h, p)
        h, hb = outproj_ln(h, y, p["w_o"], vec(ln_g[layer, 1]), vec(ln_b[layer, 1]))
        h, hb = ffn_ln(h, hb, ffn_in, ffn_out, layer, 1, vec(ln_g[layer, 2]), vec(ln_b[layer, 2]))
    return h[N_META:l][None]
```

```python
import functools

import numpy as np
import jax
import jax.numpy as jnp
from jax import lax
from jax.experimental import pallas as pl
from jax.experimental.pallas import tpu as pltpu

F32 = jnp.float32
BF16 = jnp.bfloat16

D_MODEL = 2048
DEPTH = 4
N_META = 16
D_FF = 5632
ALPHA = (2 * DEPTH) ** 0.25
LN_EPS = 1e-5
RWKV_HEAD_DIM = 64
RWKV_HEADS = D_MODEL // RWKV_HEAD_DIM
RWKV_GN_EPS = 64e-5
RG_BLOCKS = 16
D_RNN = 2688
RG_BLOCK_DIM = D_RNN // RG_BLOCKS
RG_C = 8.0
CONV_W = 4
FOX_HEAD_DIM = 128
FOX_HEADS = D_MODEL // FOX_HEAD_DIM
ML_HEADS = 8
ML_QK = D_MODEL // 2
ML_V = D_MODEL
ML_DQK = ML_QK // ML_HEADS
ML_DV = ML_V // ML_HEADS
ML_EPS = 1e-6

LANE = 128
SUBLANE = 8
ROW_ALIGN = 1280
VMEM_LIMIT = 58 * 1024 * 1024

FFN_TM, FFN_TN = 640, 512
PROJ_TM = 640
OUT_TM = 256
RWKV_PREP_TM, RWKV_PREP_TN = 640, 256
RWKV_CHUNK = 64
RG_TM = 256
RG_KWIN = 512
FOX_TQ = 640
FOX_STRIP = 16
ML_CHUNK = 640

LOG2E = float(np.log2(np.e))


def _cparams(sem):
    return pltpu.CompilerParams(dimension_semantics=sem, vmem_limit_bytes=VMEM_LIMIT)


def _ln(y, g, b):
    mu = jnp.mean(y, -1, keepdims=True)
    yc = y - mu
    var = jnp.mean(yc * yc, -1, keepdims=True)
    return yc * lax.rsqrt(var + LN_EPS) * g + b


def _softplus(z):
    return jnp.maximum(z, 0.0) + jnp.log1p(jnp.exp(-jnp.abs(z)))


def _log_sigmoid(z):
    return -_softplus(-z)


def _scan_add(x, axis):
    n = x.shape[axis]
    idx = lax.broadcasted_iota(jnp.int32, x.shape, axis)
    s = 1
    while s < n:
        x = x + jnp.where(idx >= s, pltpu.roll(x, s, axis), 0.0)
        s *= 2
    return x


def _ffn_ln_body(xb_ref, xp_ref, wg_ref, wu_ref, wo_ref, g_ref, b_ref, o_ref, ob_ref, acc_a, acc_b):
    s = pl.program_id(0)
    nf = D_FF // FFN_TN
    i, j = s // nf, s % nf
    rows_per_step = FFN_TM // (nf - 1)
    n_tiles = (pl.num_programs(0) - (nf - 1)) // nf

    def slice_rows(k):
        return pl.ds(pl.multiple_of(k * rows_per_step, rows_per_step), rows_per_step)

    def ln_slice(acc_prev):
        rows = slice_rows(jnp.minimum(j, nf - 2))
        y = _ln(ALPHA * xp_ref[...] + 0.5 * acc_prev[rows, :], g_ref[...], b_ref[...])
        o_ref[...] = y
        ob_ref[...] = y.astype(BF16)

    def step(acc_cur, acc_prev):
        @pl.when(s == 0)
        def _():
            acc_cur[...] = jnp.zeros_like(acc_cur)
            acc_prev[...] = jnp.zeros_like(acc_prev)

        @pl.when(i < n_tiles)
        def _():
            ln_slice(acc_prev)
            acc_prev[slice_rows(jnp.maximum(j - 1, 0)), :] = jnp.zeros((rows_per_step, D_MODEL), F32)
            xb = xb_ref[...]
            hg = jnp.dot(xb, wg_ref[...], preferred_element_type=F32)
            hu = jnp.dot(xb, wu_ref[...], preferred_element_type=F32)
            act = (hg * jax.nn.sigmoid(hg) * hu).astype(BF16)
            acc_cur[...] += jnp.dot(act, wo_ref[...].astype(BF16), preferred_element_type=F32)

        @pl.when(i >= n_tiles)
        def _():
            ln_slice(acc_prev)

    @pl.when(i % 2 == 0)
    def _():
        step(acc_a, acc_b)

    @pl.when(i % 2 == 1)
    def _():
        step(acc_b, acc_a)


def ffn_ln(h, hb, w_in, w_out, layer, idx, g, b):
    lp = h.shape[0]
    tm, tn = FFN_TM, FFN_TN
    nf = D_FF // tn
    nm = lp // tm
    assert tm % (nf - 1) == 0
    rows = tm // (nf - 1)
    last = nm * nf - 1
    cur = lambda s: (jnp.minimum(s // nf, nm - 1), 0)
    prev = lambda s: (jnp.where(s < nf, 0, (s // nf - 1) * (nf - 1) + jnp.minimum(s % nf, nf - 2)), 0)
    wcol = lambda s: jnp.minimum(s, last) % nf
    return pl.pallas_call(
        _ffn_ln_body,
        grid=(nm * nf + nf - 1,),
        in_specs=[
            pl.BlockSpec((tm, D_MODEL), cur),
            pl.BlockSpec((rows, D_MODEL), prev),
            pl.BlockSpec((None, None, D_MODEL, tn), lambda s: (layer, idx, 0, wcol(s))),
            pl.BlockSpec((None, None, D_MODEL, tn), lambda s: (layer, idx, 0, wcol(s) + nf)),
            pl.BlockSpec((None, None, tn, D_MODEL), lambda s: (layer, idx, wcol(s), 0)),
            pl.BlockSpec((1, D_MODEL), lambda s: (0, 0)),
            pl.BlockSpec((1, D_MODEL), lambda s: (0, 0)),
        ],
        out_specs=[pl.BlockSpec((rows, D_MODEL), prev), pl.BlockSpec((rows, D_MODEL), prev)],
        out_shape=[jax.ShapeDtypeStruct((lp, D_MODEL), F32), jax.ShapeDtypeStruct((lp, D_MODEL), BF16)],
        scratch_shapes=[pltpu.VMEM((tm, D_MODEL), F32), pltpu.VMEM((tm, D_MODEL), F32)],
        compiler_params=_cparams(("arbitrary",)),
        name="ffn_ln",
    )(hb, h, w_in, w_in, w_out, g, b)


def _proj_body(x_ref, w_ref, sc_ref, o_ref, xb_ref):
    @pl.when(pl.program_id(1) == 0)
    def _():
        xb_ref[...] = x_ref[...].astype(BF16)

    acc = jnp.dot(xb_ref[...], w_ref[...], preferred_element_type=F32)
    o_ref[...] = (acc * sc_ref[...]).astype(o_ref.dtype)


def proj(h, w, col_scale, tn, out_dtype):
    lp, n = h.shape[0], w.shape[1]
    tm = PROJ_TM
    return pl.pallas_call(
        _proj_body,
        grid=(lp // tm, n // tn),
        in_specs=[pl.BlockSpec((tm, D_MODEL), lambda i, j: (i, 0)),
                  pl.BlockSpec((D_MODEL, tn), lambda i, j: (0, j)),
                  pl.BlockSpec((1, tn), lambda i, j: (0, j))],
        out_specs=pl.BlockSpec((tm, tn), lambda i, j: (i, j)),
        out_shape=jax.ShapeDtypeStruct((lp, n), out_dtype),
        scratch_shapes=[pltpu.VMEM((tm, D_MODEL), BF16)],
        compiler_params=_cparams(("parallel", "arbitrary")),
        name="proj",
    )(h, w, col_scale)


def _outproj_ln_body(x_ref, y_ref, w_ref, g_ref, b_ref, o_ref, ob_ref):
    acc = jnp.dot(y_ref[...], w_ref[...], preferred_element_type=F32)
    out = _ln(ALPHA * x_ref[...] + acc, g_ref[...], b_ref[...])
    o_ref[...] = out
    ob_ref[...] = out.astype(BF16)


def outproj_ln(h, y, w, g, b):
    lp, k = y.shape
    tm = OUT_TM
    return pl.pallas_call(
        _outproj_ln_body,
        grid=(lp // tm,),
        in_specs=[pl.BlockSpec((tm, D_MODEL), lambda i: (i, 0)),
                  pl.BlockSpec((tm, k), lambda i: (i, 0)),
                  pl.BlockSpec((k, D_MODEL), lambda i: (0, 0)),
                  pl.BlockSpec((1, D_MODEL), lambda i: (0, 0)),
                  pl.BlockSpec((1, D_MODEL), lambda i: (0, 0))],
        out_specs=[pl.BlockSpec((tm, D_MODEL), lambda i: (i, 0))] * 2,
        out_shape=[jax.ShapeDtypeStruct((lp, D_MODEL), F32), jax.ShapeDtypeStruct((lp, D_MODEL), BF16)],
        compiler_params=_cparams(("parallel",)),
        name="outproj_ln",
    )(h, y, w, g, b)


def _rwkv_prep_body(x_ref, xp_ref, mix_ref, wrkv_ref, w1_ref, a1_ref, g1_ref, w2_ref, a2_ref, g2_ref,
                    w0_ref, a0_ref, kk_ref, ka_ref, bd_ref,
                    r_o, lw_o, k_o, v_o, kk_o, b_o, g_o,
                    xx_ref, xs_ref, tw_ref, ta_ref, tg_ref):
    i, j = pl.program_id(0), pl.program_id(1)
    tm = x_ref.shape[0]

    @pl.when(j == 0)
    def _():
        x = x_ref[...]
        prev = jnp.where(i == 0, 0.0, xp_ref[SUBLANE - 1:SUBLANE, :])
        first = lax.broadcasted_iota(jnp.int32, x.shape, 0) == 0
        xx_ref[...] = jnp.where(first, prev, pltpu.roll(x, 1, 0)) - x
        for s in range(3):
            xs_ref[s] = (x + xx_ref[...] * mix_ref[s:s + 1, :]).astype(BF16)
        xs_w = (x + xx_ref[...] * mix_ref[3:4, :]).astype(BF16)
        xs_a = (x + xx_ref[...] * mix_ref[4:5, :]).astype(BF16)
        xs_g = (x + xx_ref[...] * mix_ref[5:6, :]).astype(BF16)
        tw_ref[...] = jnp.tanh(jnp.dot(xs_w, w1_ref[...], preferred_element_type=F32)).astype(BF16)
        ta_ref[...] = jnp.dot(xs_a, a1_ref[...], preferred_element_type=F32).astype(BF16)
        tg_ref[...] = jax.nn.sigmoid(jnp.dot(xs_g, g1_ref[...], preferred_element_type=F32)).astype(BF16)

    r = jnp.dot(xs_ref[0], wrkv_ref[0], preferred_element_type=F32)
    k = jnp.dot(xs_ref[1], wrkv_ref[1], preferred_element_type=F32)
    v = jnp.dot(xs_ref[2], wrkv_ref[2], preferred_element_type=F32)
    wz = w0_ref[...] + jnp.dot(tw_ref[...], w2_ref[...], preferred_element_type=F32)
    az = a0_ref[...] + jnp.dot(ta_ref[...], a2_ref[...], preferred_element_type=F32)
    g = jnp.dot(tg_ref[...], g2_ref[...], preferred_element_type=F32)

    w_log = -_softplus(-wz) - 0.5
    a = jax.nn.sigmoid(az)
    kk = k * kk_ref[...]
    kk2 = kk * kk
    bd = bd_ref[...]
    kk2_hi = kk2.astype(BF16)
    kk2_lo = (kk2 - kk2_hi.astype(F32)).astype(BF16)
    ss = jnp.concatenate(
        [jnp.dot(jnp.concatenate([kk2_hi[:, c * LANE:(c + 1) * LANE], kk2_lo[:, c * LANE:(c + 1) * LANE]], axis=1),
                 bd, preferred_element_type=F32)
         for c in range(kk2.shape[1] // LANE)], axis=1)
    kkn = kk * lax.rsqrt(ss + 1e-12)

    r_o[...] = r
    lw_o[...] = -jnp.exp(w_log)
    k_o[...] = k * (1.0 + (a - 1.0) * ka_ref[...])
    v_o[...] = v
    kk_o[...] = kkn
    b_o[...] = kkn * a
    g_o[...] = g


def rwkv_prep(h, p):
    lp = h.shape[0]
    tm, tn = RWKV_PREP_TM, RWKV_PREP_TN
    row = lambda i, j: (i, 0)
    col = lambda i, j: (0, j)
    full = lambda i, j: (0, 0)
    lw, la, lg = p["w1"].shape[1], p["a1"].shape[1], p["g1"].shape[1]
    out = jax.ShapeDtypeStruct((lp, D_MODEL), F32)
    return pl.pallas_call(
        _rwkv_prep_body,
        grid=(lp // tm, D_MODEL // tn),
        in_specs=[
            pl.BlockSpec((tm, D_MODEL), row),
            pl.BlockSpec((SUBLANE, D_MODEL), lambda i, j: (jnp.maximum(i * (tm // SUBLANE) - 1, 0), 0)),
            pl.BlockSpec((SUBLANE, D_MODEL), full),
            pl.BlockSpec((3, D_MODEL, tn), lambda i, j: (0, 0, j)),
            pl.BlockSpec((D_MODEL, lw), full),
            pl.BlockSpec((D_MODEL, la), full),
            pl.BlockSpec((D_MODEL, lg), full),
            pl.BlockSpec((lw, tn), col),
            pl.BlockSpec((la, tn), col),
            pl.BlockSpec((lg, tn), col),
            pl.BlockSpec((1, tn), col),
            pl.BlockSpec((1, tn), col),
            pl.BlockSpec((1, tn), col),
            pl.BlockSpec((1, tn), col),
            pl.BlockSpec((2 * LANE, LANE), full),
        ],
        out_specs=[pl.BlockSpec((tm, tn), lambda i, j: (i, j))] * 7,
        out_shape=[out] * 7,
        scratch_shapes=[pltpu.VMEM((tm, D_MODEL), F32), pltpu.VMEM((3, tm, D_MODEL), BF16),
                        pltpu.VMEM((tm, lw), BF16), pltpu.VMEM((tm, la), BF16), pltpu.VMEM((tm, lg), BF16)],
        compiler_params=_cparams(("parallel", "arbitrary")),
        name="rwkv_prep",
    )(h, h, p["mix"], p["w_rkv"], p["w1"], p["a1"], p["g1"], p["w2"], p["a2"], p["g2"],
      p["w0"], p["a0"], p["k_k"], p["k_a"], p["bd"])


RWKV_GROUP = 4
RWKV_GW = RWKV_GROUP * RWKV_HEAD_DIM
RWKV_SOLVE_BASE = 8

_NN = ((1,), (0,))
_NT = ((1,), (1,))
_TN = ((0,), (0,))


def _mm(a, b, dims):
    return lax.dot_general(a.astype(BF16), b.astype(BF16), (dims, ((), ())), preferred_element_type=F32)


def _rwkv_chunk_body(r_ref, lw_ref, k_ref, v_ref, kk_ref, b_ref, g_ref, rk_ref, gng_ref, gnb_ref, o_ref, s_ref):
    c = r_ref.shape[0]
    n = RWKV_HEAD_DIM
    gw = RWKV_GW

    @pl.when(pl.program_id(0) == 0)
    def _():
        s_ref[...] = jnp.zeros_like(s_ref)

    lw = lw_ref[...]
    cum = _scan_add(lw, 0)
    e_pos = jnp.exp(cum)
    e_neg = jnp.exp(-cum)
    e_end = e_pos[c - 1:c, :]
    at_all = -kk_ref[...] * jnp.exp(cum - lw)
    rt_all = r_ref[...] * e_pos
    bb_all = b_ref[...] * e_neg
    kb_all = k_ref[...] * e_neg
    rkr_all = r_ref[...] * k_ref[...] * rk_ref[...]

    blk = (lax.broadcasted_iota(jnp.int32, (gw, gw), 0) // n == lax.broadcasted_iota(jnp.int32, (gw, gw), 1) // n)
    blk = blk.astype(F32).astype(BF16)
    row = lax.broadcasted_iota(jnp.int32, (c, gw), 0)
    src = lax.broadcasted_iota(jnp.int32, (c, gw), 1) % n
    strict = src < row
    incl = src <= row

    def bdiag(x):
        return jnp.tile(x.astype(BF16), (RWKV_GROUP, 1)) * blk

    def seg_sums(xs):
        hi = [x.astype(BF16) for x in xs]
        lo = [(x - h_.astype(F32)).astype(BF16) for x, h_ in zip(xs, hi)]
        tot = jnp.dot(jnp.concatenate(hi + lo, axis=0), blk, preferred_element_type=F32)
        rows = xs[0].shape[0]
        return [tot[i * rows:(i + 1) * rows] + tot[(len(xs) + i) * rows:(len(xs) + i + 1) * rows]
                for i in range(len(xs))]

    groups = [slice(g * gw, (g + 1) * gw) for g in range(D_MODEL // gw)]
    v_g = [v_ref[:, sl] for sl in groups]
    lhs = [jnp.concatenate([at_all[:, sl], rt_all[:, sl]], axis=0).astype(BF16) for sl in groups]
    g_b = [_mm(a, bdiag(bb_all[:, sl]), _NT) for a, sl in zip(lhs, groups)]
    g_k = [_mm(a, bdiag(kb_all[:, sl]), _NT) for a, sl in zip(lhs, groups)]
    s0 = [s_ref[g] for g in range(len(groups))]
    z = [_mm(a, bdiag(s_), _NT) for a, s_ in zip(lhs, s0)]
    a_ab = [jnp.where(strict, gm[:c], 0.0) for gm in g_b]
    a_ak = [jnp.where(strict, gm[:c], 0.0) for gm in g_k]
    a_r = [jnp.concatenate([jnp.where(incl, gb[c:], 0.0), jnp.where(incl, gk[c:], 0.0)], axis=1).astype(BF16)
           for gb, gk in zip(g_b, g_k)]
    bd_v = [bdiag(v_) for v_ in v_g]
    rhs = [z_[:c] + _mm(a, bv, _NN) for z_, a, bv in zip(z, a_ak, bd_v)]
    def within(size):
        return (row // size) == (src // size)

    tinv = [jnp.where(within(RWKV_SOLVE_BASE), a, 0.0) for a in a_ab]
    pw = tinv
    tinv = [(src == row).astype(F32) + t_ for t_ in tinv]
    span = 2
    while span < RWKV_SOLVE_BASE:
        pw = [_mm(p_, bdiag(p_), _NN) for p_ in pw]
        tinv = [t_ + _mm(p_, bdiag(t_), _NN) for t_, p_ in zip(tinv, pw)]
        span *= 2
    size = RWKV_SOLVE_BASE
    while size < c:
        link = within(2 * size) & jnp.logical_not(within(size))
        bd_t = [bdiag(t_) for t_ in tinv]
        tl = [_mm(t_, bdiag(jnp.where(link, a, 0.0)), _NN) for t_, a in zip(tinv, a_ab)]
        tinv = [t_ + _mm(x_, b_, _NN) for t_, x_, b_ in zip(tinv, tl, bd_t)]
        size *= 2
    u = [_mm(t_, bdiag(r_), _NN) for t_, r_ in zip(tinv, rhs)]
    y = [z_[c:] + _mm(a, jnp.concatenate([bdiag(u_), bv], axis=0), _NN)
         for z_, a, u_, bv in zip(z, a_r, u, bd_v)]
    for g, sl in enumerate(groups):
        e_end_g = e_end[:, sl]
        uv = jnp.concatenate([u[g], v_g[g]], axis=0)
        bk = jnp.concatenate([bb_all[:, sl], kb_all[:, sl]], axis=0) * e_end_g
        full = _mm(uv, bk, _TN) * blk.astype(F32)
        s_ref[g] = s0[g] * e_end_g + functools.reduce(
            lambda p_, q_: p_ + q_, [full[q * n:(q + 1) * n, :] for q in range(RWKV_GROUP)])
    sums = seg_sums(y + [rkr_all[:, sl] for sl in groups])
    yc = [y_ - m_ * (1.0 / n) for y_, m_ in zip(y, sums[:len(groups)])]
    var = seg_sums([t * t for t in yc])
    for g, sl in enumerate(groups):
        yn = yc[g] * lax.rsqrt(var[g] * (1.0 / n) + RWKV_GN_EPS) * gng_ref[:, sl] + gnb_ref[:, sl]
        bonus = sums[len(groups) + g] * v_g[g]
        o_ref[:, sl] = ((yn + bonus) * g_ref[:, sl]).astype(o_ref.dtype)


def rwkv_chunk(r, lw, k, v, kk, b, g, r_k, gn_g, gn_b):
    lp = r.shape[0]
    c = RWKV_CHUNK
    blk = pl.BlockSpec((c, D_MODEL), lambda i: (i, 0))
    vec = pl.BlockSpec((1, D_MODEL), lambda i: (0, 0))
    return pl.pallas_call(
        _rwkv_chunk_body,
        grid=(lp // c,),
        in_specs=[blk] * 7 + [vec] * 3,
        out_specs=blk,
        out_shape=jax.ShapeDtypeStruct((lp, D_MODEL), BF16),
        scratch_shapes=[pltpu.VMEM((D_MODEL // RWKV_GW, RWKV_HEAD_DIM, RWKV_GW), F32)],
        compiler_params=_cparams(("arbitrary",)),
        name="rwkv_chunk",
    )(r, lw, k, v, kk, b, g, r_k, gn_g, gn_b)


def _pad_rows(a, rows):
    return jnp.pad(a, ((0, rows - a.shape[0]), (0, 0)))


def _pad_cols(a, cols):
    return jnp.pad(a, ((0, 0), (0, cols - a.shape[1])))


def rwkv_params(mix, w_rkv, w0, w1, w2, a0, a1, a2, g1, g2, k_k, k_a, r_k, gn_g, gn_b, w_o):
    lora = LANE
    bd = np.kron(np.eye(LANE // RWKV_HEAD_DIM, dtype=np.float32),
                 np.ones((RWKV_HEAD_DIM, RWKV_HEAD_DIM), np.float32))
    bd = np.concatenate([bd, bd], axis=0)
    row = lambda t: t.reshape(1, D_MODEL)
    return dict(
        mix=_pad_rows(mix, SUBLANE), w_rkv=w_rkv.astype(BF16),
        w1=_pad_cols(w1, lora).astype(BF16), w2=_pad_rows(w2, lora).astype(BF16),
        a1=_pad_cols(a1, lora).astype(BF16), a2=_pad_rows(a2, lora).astype(BF16),
        g1=g1.astype(BF16), g2=g2.astype(BF16),
        w0=row(w0), a0=row(a0), k_k=row(k_k), k_a=row(k_a), bd=jnp.asarray(bd, BF16),
        r_k=row(r_k), gn_g=row(gn_g), gn_b=row(gn_b), w_o=w_o.astype(BF16))


def rwkv_mixer(h, p):
    r, lw, k, v, kk, b, g = rwkv_prep(h, p)
    return rwkv_chunk(r, lw, k, v, kk, b, g, p["r_k"], p["gn_g"], p["gn_b"])


def _rg_windows():
    starts = []
    for c in range(D_RNN // LANE):
        first_block = (c * LANE) // RG_BLOCK_DIM
        ks = (first_block * RG_BLOCK_DIM) // LANE * LANE
        starts.append(min(ks, D_RNN - RG_KWIN))
    return starts


def _rg_compact(w):
    eye = jnp.eye(RG_BLOCKS, dtype=w.dtype)
    dense = (eye[:, None, :, None] * w[:, :, None, :]).reshape(D_RNN, D_RNN)
    tiles = [dense[ks:ks + RG_KWIN, c * LANE:(c + 1) * LANE] for c, ks in enumerate(_rg_windows())]
    return jnp.stack(tiles).astype(BF16)


def _gelu_tanh(x):
    return 0.5 * x * (1.0 + jnp.tanh(np.sqrt(2.0 / np.pi).astype(np.float32) * (x + 0.044715 * (x * x * x))))


def _rg_body(gate_ref, u_ref, up_ref, cw_ref, cb_ref, wa_ref, wx_ref, ba_ref, bx_ref, lam_ref, o_ref,
             ue_ref, carry_ref):
    i = pl.program_id(0)
    tm = u_ref.shape[0]

    @pl.when(i == 0)
    def _():
        carry_ref[...] = jnp.zeros_like(carry_ref)

    u = u_ref[...]
    ue_ref[0:SUBLANE, :] = jnp.where(i == 0, 0.0, up_ref[...])
    ue_ref[SUBLANE:, :] = u
    uc = cb_ref[...] + u * cw_ref[CONV_W - 1:CONV_W, :]
    for d in range(1, CONV_W):
        uc = uc + ue_ref[pl.ds(SUBLANE - d, tm), :] * cw_ref[CONV_W - 1 - d:CONV_W - d, :]
    ucb = uc.astype(BF16)
    za, zx = [], []
    for c, ks in enumerate(_rg_windows()):
        win = ucb[:, ks:ks + RG_KWIN]
        za.append(jnp.dot(win, wa_ref[c], preferred_element_type=F32))
        zx.append(jnp.dot(win, wx_ref[c], preferred_element_type=F32))
    r = jax.nn.sigmoid(jnp.concatenate(za, axis=1) + ba_ref[...])
    ig = jax.nn.sigmoid(jnp.concatenate(zx, axis=1) + bx_ref[...])
    log_a = -RG_C * r * _softplus(-lam_ref[...])
    a = jnp.exp(log_a)
    w2 = -2.0 * jnp.tanh(log_a)
    q = w2 * (1.0 + 0.5 * w2)
    b = jnp.where(q > 0.0, w2 * lax.rsqrt(q), 0.0) * (ig * uc)
    groups = tm // SUBLANE
    a = a.reshape(groups, SUBLANE, D_RNN)
    b = b.reshape(groups, SUBLANE, D_RNN)
    sub = lax.broadcasted_iota(jnp.int32, a.shape, 1)
    s = 1
    while s < SUBLANE:
        keep = sub >= s
        b = jnp.where(keep, a * pltpu.roll(b, s, 1) + b, b)
        a = jnp.where(keep, a * pltpu.roll(a, s, 1), a)
        s *= 2
    carry = carry_ref[0:1, :]
    for g in range(groups):
        hg = a[g] * carry + b[g]
        rows = pl.ds(g * SUBLANE, SUBLANE)
        o_ref[rows, :] = (_gelu_tanh(gate_ref[rows, :]) * hg).astype(o_ref.dtype)
        carry = hg[SUBLANE - 1:SUBLANE, :]
    carry_ref[...] = jnp.broadcast_to(carry, carry_ref.shape)


def rg_params(w_in, conv_w, conv_b, w_a, b_a, w_x, b_x, lam, w_o):
    row = lambda t: t.reshape(1, D_RNN)
    return dict(w_in=w_in.astype(BF16), conv_w=_pad_rows(conv_w, SUBLANE), conv_b=row(conv_b),
                w_a=_rg_compact(w_a), w_x=_rg_compact(w_x), b_a=row(b_a), b_x=row(b_x),
                lam=row(lam), w_o=w_o.astype(BF16), ones=jnp.ones((1, 2 * D_RNN), F32))


def rg_mixer(h, p):
    lp = h.shape[0]
    tm = RG_TM
    gu = proj(h, p["w_in"], p["ones"], 768, F32)
    ntile = D_RNN // LANE
    vec = pl.BlockSpec((1, D_RNN), lambda i: (0, 0))
    wspec = pl.BlockSpec((ntile, RG_KWIN, LANE), lambda i: (0, 0, 0))
    return pl.pallas_call(
        _rg_body,
        grid=(lp // tm,),
        in_specs=[pl.BlockSpec((tm, D_RNN), lambda i: (i, 0)),
                  pl.BlockSpec((tm, D_RNN), lambda i: (i, 1)),
                  pl.BlockSpec((SUBLANE, D_RNN), lambda i: (jnp.maximum(i * (tm // SUBLANE) - 1, 0), 1)),
                  pl.BlockSpec((SUBLANE, D_RNN), lambda i: (0, 0)),
                  vec, wspec, wspec, vec, vec, vec],
        out_specs=pl.BlockSpec((tm, D_RNN), lambda i: (i, 0)),
        out_shape=jax.ShapeDtypeStruct((lp, D_RNN), BF16),
        scratch_shapes=[pltpu.VMEM((tm + SUBLANE, D_RNN), F32), pltpu.VMEM((SUBLANE, D_RNN), F32)],
        compiler_params=_cparams(("arbitrary",)),
        name="rg_lru",
    )(gu, gu, gu, p["conv_w"], p["conv_b"], p["w_a"], p["w_x"], p["b_a"], p["b_x"], p["lam"])


def _split3(x):
    hi = x.astype(BF16).astype(F32)
    mid = (x - hi).astype(BF16).astype(F32)
    lo = (x - hi - mid).astype(BF16).astype(F32)
    return hi, mid, lo


def _fox_gate_body(x_ref, w_ref, bf_ref, cc_ref, qx_ref, kx_ref, carry_ref):
    @pl.when(pl.program_id(0) == 0)
    def _():
        carry_ref[...] = jnp.zeros_like(carry_ref)

    tm = x_ref.shape[0]
    fl = jnp.dot(x_ref[...].astype(BF16), w_ref[...], preferred_element_type=F32)
    c = _scan_add(_log_sigmoid(fl + bf_ref[...]), 0) + carry_ref[0:1, :]
    carry_ref[...] = jnp.broadcast_to(c[tm - 1:, :], carry_ref.shape)
    c = c * LOG2E
    cc_ref[...] = jnp.broadcast_to(c[0:1, :], cc_ref.shape)
    fine = c - c[0:1, :]
    lane = lax.broadcasted_iota(jnp.int32, (tm, LANE), 1)
    ones_mid = jnp.where(lane < 6, 1.0, 0.0)
    for h in range(FOX_HEADS):
        hi, mid, lo = [jnp.broadcast_to(t, (tm, LANE)) for t in _split3(fine[:, h:h + 1])]
        qx = jnp.where(lane == 0, hi, jnp.where(lane == 1, mid, lo))
        kx = -jnp.where(lane == 3, hi, jnp.where(lane == 4, mid, lo))
        qx_ref[:, h * LANE:(h + 1) * LANE] = jnp.where(lane < 3, qx, ones_mid).astype(BF16)
        kx_ref[:, h * LANE:(h + 1) * LANE] = jnp.where(lane < 3, 1.0, jnp.where(lane < 6, kx, 0.0)).astype(BF16)


def fox_gate(h, w_f, b_f):
    lp = h.shape[0]
    tm = FOX_TQ
    nt = lp // tm
    wide = jax.ShapeDtypeStruct((lp, FOX_HEADS * LANE), BF16)
    return pl.pallas_call(
        _fox_gate_body,
        grid=(nt,),
        in_specs=[pl.BlockSpec((tm, D_MODEL), lambda i: (i, 0)),
                  pl.BlockSpec((D_MODEL, LANE), lambda i: (0, 0)),
                  pl.BlockSpec((1, LANE), lambda i: (0, 0))],
        out_specs=[pl.BlockSpec((SUBLANE, LANE), lambda i: (i, 0)),
                   pl.BlockSpec((tm, FOX_HEADS * LANE), lambda i: (i, 0)),
                   pl.BlockSpec((tm, FOX_HEADS * LANE), lambda i: (i, 0))],
        out_shape=[jax.ShapeDtypeStruct((nt * SUBLANE, LANE), F32), wide, wide],
        scratch_shapes=[pltpu.VMEM((SUBLANE, LANE), F32)],
        compiler_params=_cparams(("arbitrary",)),
        name="fox_gate",
    )(h, w_f, b_f)


def _fox_attn_body(cc_ref, q_ref, qx_ref, k_ref, kx_ref, v_ref, o_ref,
                   s_ref, p_ref, mx_ref, d_ref, m_ref, corr_ref, acc_ref):
    h, i = pl.program_id(0), pl.program_id(1)
    tq = q_ref.shape[0]
    nl = tq // LANE
    q_aug = jnp.concatenate([q_ref[...], qx_ref[...]], axis=1)
    ones_col = (lax.broadcasted_iota(jnp.int32, (tq, LANE), 1) == 0).astype(BF16)
    strips = list(range(0, tq, FOX_STRIP))

    def tile_rows(j):
        return pl.ds(pl.multiple_of(j * tq, tq), tq)

    half = tq // 2

    def logits(j, slot, split=False):
        rows = tile_rows(j)
        k_aug = jnp.concatenate([k_ref[rows, :], kx_ref[rows, :]], axis=1)
        if split:
            for r0 in (0, half):
                s_ref[slot, r0:r0 + half, :] = lax.dot_general(q_aug[r0:r0 + half], k_aug, (_NT, ((), ())),
                                                               preferred_element_type=F32)
        else:
            s_ref[slot] = lax.dot_general(q_aug, k_aug, (_NT, ((), ())), preferred_element_type=F32)

    def softmax(j, slot, diagonal):
        delta = cc_ref[h, i] - cc_ref[h, j]
        for r0 in strips:
            rows = pl.ds(r0, FOX_STRIP)
            t = s_ref[slot, rows, :]
            if diagonal:
                row = r0 + lax.broadcasted_iota(jnp.int32, t.shape, 0)
                col = lax.broadcasted_iota(jnp.int32, t.shape, 1)
                t = jnp.where(col <= row, t, -jnp.inf)
                s_ref[slot, rows, :] = t
            mx_ref[rows, :] = functools.reduce(jnp.maximum, [t[:, c * LANE:(c + 1) * LANE] for c in range(nl)])
        m_old = m_ref[...]
        m_new = jnp.maximum(m_old, jnp.max(mx_ref[...], -1, keepdims=True) + delta)
        m_ref[...] = m_new
        corr_ref[slot] = jnp.exp2(m_old - m_new)
        d_ref[...] = jnp.broadcast_to(delta - m_new, d_ref.shape)
        for r0 in strips:
            rows = pl.ds(r0, FOX_STRIP)
            e = s_ref[slot, rows, :] + jnp.tile(d_ref[rows, :], (1, nl))
            p_ref[slot, rows, :] = jnp.exp2(e).astype(BF16)

    def accumulate(j, slot, split=False):
        vb = jnp.concatenate([v_ref[tile_rows(j), :], ones_col], axis=1)
        if split:
            for r0 in (0, half):
                rows = pl.ds(r0, half)
                acc_ref[rows, :] = corr_ref[slot, rows, :] * acc_ref[rows, :] + jnp.dot(
                    p_ref[slot, rows, :], vb, preferred_element_type=F32)
        else:
            acc_ref[...] = corr_ref[slot] * acc_ref[...] + jnp.dot(p_ref[slot], vb, preferred_element_type=F32)

    m_ref[...] = jnp.full(m_ref.shape, -jnp.inf, F32)
    acc_ref[...] = jnp.zeros_like(acc_ref)
    p_ref[1] = jnp.zeros(p_ref.shape[1:], BF16)
    corr_ref[1] = jnp.ones(corr_ref.shape[1:], F32)
    logits(0, 0, split=True)

    def pair(a, carry):
        j = 2 * a
        accumulate(jnp.maximum(j - 1, 0), 1)
        softmax(j, 0, False)
        logits(j + 1, 1)
        accumulate(j, 0)
        softmax(j + 1, 1, False)
        logits(j + 2, 0)
        return carry

    lax.fori_loop(0, i // 2, pair, 0)

    @pl.when(i % 2 == 0)
    def _():
        accumulate(jnp.maximum(i - 1, 0), 1)
        softmax(i, 0, True)
        accumulate(i, 0, split=True)

    @pl.when(i % 2 == 1)
    def _():
        accumulate(jnp.maximum(i - 2, 0), 1)
        softmax(i - 1, 0, False)
        logits(i, 1)
        accumulate(i - 1, 0)
        softmax(i, 1, True)
        accumulate(i, 1, split=True)

    acc = acc_ref[...]
    o_ref[...] = (acc[:, :FOX_HEAD_DIM] / acc[:, FOX_HEAD_DIM:FOX_HEAD_DIM + 1]).astype(o_ref.dtype)


def fox_attn(qkv, qx, kx, cc):
    lp = qkv.shape[0]
    tq = FOX_TQ
    hh = FOX_HEADS
    hd = FOX_HEAD_DIM
    return pl.pallas_call(
        _fox_attn_body,
        grid=(hh, lp // tq),
        in_specs=[pl.BlockSpec(memory_space=pltpu.SMEM),
                  pl.BlockSpec((tq, hd), lambda h, i: (i, h)),
                  pl.BlockSpec((tq, LANE), lambda h, i: (i, h)),
                  pl.BlockSpec((lp, hd), lambda h, i: (0, hh + h)),
                  pl.BlockSpec((lp, LANE), lambda h, i: (0, h)),
                  pl.BlockSpec((lp, hd), lambda h, i: (0, 2 * hh + h))],
        out_specs=pl.BlockSpec((tq, hd), lambda h, i: (i, h)),
        out_shape=jax.ShapeDtypeStruct((lp, D_MODEL), BF16),
        scratch_shapes=[pltpu.VMEM((2, tq, tq), F32), pltpu.VMEM((2, tq, tq), BF16), pltpu.VMEM((tq, LANE), F32),
                        pltpu.VMEM((tq, LANE), F32), pltpu.VMEM((tq, 1), F32), pltpu.VMEM((2, tq, 1), F32),
                        pltpu.VMEM((tq, hd + LANE), F32)],
        compiler_params=_cparams(("parallel", "arbitrary")),
        name="fox_attn",
    )(cc, qkv, qx, qkv, kx, qkv)


def fox_params(w_in, b_f, w_o):
    q_scale = jnp.full((1, D_MODEL), LOG2E * FOX_HEAD_DIM ** -0.5, F32)
    return dict(w_qkv=w_in[:, :3 * D_MODEL].astype(BF16),
                qkv_scale=jnp.concatenate([q_scale, jnp.ones((1, 2 * D_MODEL), F32)], axis=1),
                w_f=_pad_cols(w_in[:, 3 * D_MODEL:], LANE).astype(BF16),
                b_f=_pad_cols(b_f.reshape(1, FOX_HEADS), LANE), w_o=w_o.astype(BF16))


def fox_mixer(h, p):
    qkv = proj(h, p["w_qkv"], p["qkv_scale"], 768, BF16)
    cc, qx, kx = fox_gate(h, p["w_f"], p["b_f"])
    cc = cc[::SUBLANE, :FOX_HEADS].T
    return fox_attn(qkv, qx, kx, cc)


def _mlstm_body(x_ref, gc_ref, gr_ref, bc_ref, br_ref, ng_ref, o_ref, c_ref, n_ref, m_ref):
    cm = x_ref.shape[0]
    hh = ML_HEADS

    @pl.when(pl.program_id(0) == 0)
    def _():
        c_ref[...] = jnp.zeros_like(c_ref)
        n_ref[...] = jnp.zeros_like(n_ref)
        m_ref[...] = jnp.zeros_like(m_ref)

    gc = gc_ref[...] + bc_ref[...]
    bcum_c = _scan_add(_log_sigmoid(gc), 0)
    gr = gr_ref[...] + br_ref[:, 0:1]
    bcum_r = _scan_add(_log_sigmoid(gr), 1)
    row = lax.broadcasted_iota(jnp.int32, (cm, cm), 0)
    col = lax.broadcasted_iota(jnp.int32, (cm, cm), 1)
    causal = col <= row

    for h in range(hh):
        b_col = bcum_c[:, hh + h:hh + h + 1]
        ig_col = gc[:, h:h + 1]
        b_row = bcum_r[hh + h:hh + h + 1, :]
        ig_row = gr[h:h + 1, :]
        m_prev = m_ref[h][0:1, 0:1]
        q = x_ref[:, h * ML_DQK:(h + 1) * ML_DQK].astype(BF16)
        k = x_ref[:, ML_QK + h * ML_DQK:ML_QK + (h + 1) * ML_DQK] * (ML_DQK ** -0.5)
        v = x_ref[:, 2 * ML_QK + h * ML_DV:2 * ML_QK + (h + 1) * ML_DV].astype(BF16)
        og = x_ref[:, 2 * ML_QK + ML_V + h * ML_DV:2 * ML_QK + ML_V + (h + 1) * ML_DV]

        dmat = jnp.where(causal, b_col - b_row + ig_row, -jnp.inf)
        inter = b_col + m_prev
        m_t = jnp.maximum(inter, jnp.max(dmat, -1, keepdims=True))
        w_intra = jnp.exp(dmat - m_t)
        w_inter = jnp.exp(inter - m_t)
        s = lax.dot_general(q, k.astype(BF16), (_NT, ((), ())), preferred_element_type=F32) * w_intra
        c_st = c_ref[h]
        n_st = n_ref[h][0:1, :]
        num = (jnp.dot(s.astype(BF16), v, preferred_element_type=F32)
               + w_inter * jnp.dot(q, c_st.astype(BF16), preferred_element_type=F32))
        den = jnp.sum(s, -1, keepdims=True) + w_inter * jnp.sum(q.astype(F32) * n_st, -1, keepdims=True)
        hv = num / jnp.maximum(jnp.abs(den), jnp.exp(-m_t))

        b_end = b_col[cm - 1:cm, :]
        d_end = b_end - b_col + ig_col
        m_new = jnp.maximum(b_end + m_prev, jnp.max(d_end, 0, keepdims=True))
        kw = k * jnp.exp(d_end - m_new)
        carry = jnp.exp(b_end + m_prev - m_new)
        c_ref[h] = carry * c_st + lax.dot_general(kw.astype(BF16), v, (_TN, ((), ())), preferred_element_type=F32)
        n_ref[h] = jnp.broadcast_to(carry * n_st + jnp.sum(kw, 0, keepdims=True), n_ref.shape[1:])
        m_ref[h] = jnp.broadcast_to(m_new, m_ref.shape[1:])

        hn = hv * lax.rsqrt(jnp.mean(hv * hv, -1, keepdims=True) + ML_EPS)
        sl = slice(h * ML_DV, (h + 1) * ML_DV)
        o_ref[:, sl] = (hn * ng_ref[:, sl] * jax.nn.sigmoid(og)).astype(o_ref.dtype)


def mlstm_params(w_in, b_if, norm_g, w_o):
    nmain = 2 * ML_QK + 2 * ML_V
    b_c = _pad_cols(b_if.reshape(1, 2 * ML_HEADS), LANE)
    b_r = jnp.broadcast_to(b_if.reshape(2 * ML_HEADS, 1), (2 * ML_HEADS, LANE))
    return dict(w_all=_pad_cols(w_in, nmain + LANE).astype(BF16), ones=jnp.ones((1, nmain + LANE), F32),
                b_c=b_c, b_r=b_r, norm_g=norm_g.reshape(1, ML_V), w_o=w_o.astype(BF16))


def mlstm_mixer(h, p):
    lp = h.shape[0]
    cm = ML_CHUNK
    nmain = 2 * ML_QK + 2 * ML_V
    main = proj(h, p["w_all"], p["ones"], 896, F32)
    gates_r = main[:, nmain:nmain + 2 * ML_HEADS].T
    return pl.pallas_call(
        _mlstm_body,
        grid=(lp // cm,),
        in_specs=[pl.BlockSpec((cm, nmain), lambda i: (i, 0)),
                  pl.BlockSpec((cm, LANE), lambda i: (i, nmain // LANE)),
                  pl.BlockSpec((2 * ML_HEADS, cm), lambda i: (0, i)),
                  pl.BlockSpec((1, LANE), lambda i: (0, 0)),
                  pl.BlockSpec((2 * ML_HEADS, LANE), lambda i: (0, 0)),
                  pl.BlockSpec((1, ML_V), lambda i: (0, 0))],
        out_specs=pl.BlockSpec((cm, ML_V), lambda i: (i, 0)),
        out_shape=jax.ShapeDtypeStruct((lp, ML_V), BF16),
        scratch_shapes=[pltpu.VMEM((ML_HEADS, ML_DQK, ML_DV), F32),
                        pltpu.VMEM((ML_HEADS, SUBLANE, ML_DQK), F32),
                        pltpu.VMEM((ML_HEADS, SUBLANE, LANE), F32)],
        compiler_params=_cparams(("arbitrary",)),
        name="mlstm",
    )(main, main, gates_r, p["b_c"], p["b_r"], p["norm_g"])


def kernel(x, meta_tokens, ln_g, ln_b, ffn_in, ffn_out, rwkv_mix, rwkv_w_rkv, rwkv_w0, rwkv_w1, rwkv_w2, rwkv_a0, rwkv_a1, rwkv_a2, rwkv_g1, rwkv_g2, rwkv_k_k, rwkv_k_a, rwkv_r_k, rwkv_gn_g, rwkv_gn_b, rwkv_w_o, rg_w_in, rg_conv_w, rg_conv_b, rg_w_a, rg_b_a, rg_w_x, rg_b_x, rg_lambda, rg_w_o, fox_w_in, fox_b_f, fox_w_o, ml_w_in, ml_b_if, ml_norm_g, ml_w_o):
    batch, seq, _ = x.shape
    assert batch == 1
    l = seq + N_META
    lp = -(-l // ROW_ALIGN) * ROW_ALIGN
    h = jnp.concatenate([meta_tokens.astype(x.dtype), x[0], jnp.zeros((lp - l, D_MODEL), x.dtype)], axis=0)

    ffn_in_b = ffn_in.astype(BF16)
    rw = rwkv_params(rwkv_mix, rwkv_w_rkv, rwkv_w0, rwkv_w1, rwkv_w2, rwkv_a0, rwkv_a1, rwkv_a2, rwkv_g1,
                     rwkv_g2, rwkv_k_k, rwkv_k_a, rwkv_r_k, rwkv_gn_g, rwkv_gn_b, rwkv_w_o)
    rg = rg_params(rg_w_in, rg_conv_w, rg_conv_b, rg_w_a, rg_b_a, rg_w_x, rg_b_x, rg_lambda, rg_w_o)
    fx = fox_params(fox_w_in, fox_b_f, fox_w_o)
    ml = mlstm_params(ml_w_in, ml_b_if, ml_norm_g, ml_w_o)
    mixers = ((rwkv_mixer, rw), (rg_mixer, rg), (fox_mixer, fx), (mlstm_mixer, ml))

    vec = lambda t: t.reshape(1, D_MODEL)
    hb = h.astype(BF16)
    for layer in range(DEPTH):
        h, hb = ffn_ln(h, hb, ffn_in_b, ffn_out, layer, 0, vec(ln_g[layer, 0]), vec(ln_b[layer, 0]))
        mixer, p = mixers[layer % len(mixers)]
        y = mixer(h, p)
        h, hb = outproj_ln(h, y, p["w_o"], vec(ln_g[layer, 1]), vec(ln_b[layer, 1]))
        h, hb = ffn_ln(h, hb, ffn_in_b, ffn_out, layer, 1, vec(ln_g[layer, 2]), vec(ln_b[layer, 2]))
    return h[N_META:l][None]
```
